```python
import math
import jax, jax.numpy as jnp
from jax import lax
import numpy as np

D_MODEL = 4096
BATCH = 2
SEQ = 8192
DEPTH = 1

MEM_LEN = 256

SSD_WIDTH = D_MODEL
SSD_HEAD_DIM = 64
SSD_HEADS = SSD_WIDTH // SSD_HEAD_DIM
SSD_GROUPS = 8
SSD_HEADS_PER_GROUP = SSD_HEADS // SSD_GROUPS
SSD_STATE = 128
SSD_CONV = 4
SSD_CHUNK = 128
SSD_CONV_DIM = SSD_WIDTH + 2 * SSD_GROUPS * SSD_STATE
SSD_DT_MIN = 1e-3
SSD_DT_MAX = 1e-1

LRU_WIDTH = D_MODEL
LRU_BLOCKS = 16
LRU_BLOCK_DIM = LRU_WIDTH // LRU_BLOCKS
LRU_CONV = 4
LRU_C = 8.0

XATTN_HEADS = 4
XATTN_HEAD_DIM = 128
XATTN_WIDTH = XATTN_HEADS * XATTN_HEAD_DIM

PEER_HEADS = 8
PEER_N_KEYS = 128
PEER_N_EXPERTS = PEER_N_KEYS * PEER_N_KEYS
PEER_KEY_DIM = 128
PEER_TOPK = 16
PEER_TOKEN_BLOCK = 64

DEEPNORM_ALPHA = (2.0 * DEPTH) ** 0.25
DEEPNORM_BETA = (8.0 * DEPTH) ** -0.25
LN_EPS = 1e-5
RMS_EPS = 1e-6

IN_Z = SSD_WIDTH
IN_XBC = SSD_CONV_DIM
IN_DT = SSD_HEADS
IN_LRU_GATE = LRU_WIDTH
IN_LRU_X = LRU_WIDTH
IN_GATE = D_MODEL
IN_COLS = IN_Z + IN_XBC + IN_DT + IN_LRU_GATE + IN_LRU_X + 2 * IN_GATE
IN_SPLIT_POINTS = (IN_Z,
                   IN_Z + IN_XBC,
                   IN_Z + IN_XBC + IN_DT,
                   IN_Z + IN_XBC + IN_DT + IN_LRU_GATE,
                   IN_Z + IN_XBC + IN_DT + IN_LRU_GATE + IN_LRU_X,
                   IN_Z + IN_XBC + IN_DT + IN_LRU_GATE + IN_LRU_X + IN_GATE)

kernel_name = "hybrid_ssd_rglru_peer_deepnorm"


def layer_norm(x, g, b):
    xf = x.astype(jnp.float32)
    mu = jnp.mean(xf, axis=-1, keepdims=True)
    var = jnp.mean(jnp.square(xf - mu), axis=-1, keepdims=True)
    y = (xf - mu) * lax.rsqrt(var + LN_EPS)
    return (y * g + b).astype(x.dtype)


def causal_depthwise_conv(x, w, b):
    k = w.shape[0]
    y = lax.conv_general_dilated(
        x, w[:, None, :], window_strides=(1,), padding=[(k - 1, 0)],
        dimension_numbers=("NWC", "WIO", "NWC"), feature_group_count=x.shape[-1])
    return y + b


def ssd_chunked(xh, dt, a, bm, cm):
    b, s = xh.shape[:2]
    nc, q = s // SSD_CHUNK, SSD_CHUNK
    g, j, p, n = SSD_GROUPS, SSD_HEADS_PER_GROUP, SSD_HEAD_DIM, SSD_STATE
    x = (xh * dt[..., None]).reshape(b, nc, q, g, j, p)
    bc = bm.reshape(b, nc, q, g, n)
    cc = cm.reshape(b, nc, q, g, n)
    acs = jnp.cumsum((dt * a).reshape(b, nc, q, g, j), axis=2)
    acs = jnp.moveaxis(acs, 2, -1)
    causal = jnp.tril(jnp.ones((q, q), dtype=bool))
    seg = acs[..., :, None] - acs[..., None, :]
    lmat = jnp.exp(jnp.where(causal, seg, -jnp.inf))
    cb = jnp.einsum("bclgn,bcsgn->bcgls", cc, bc)
    y_diag = jnp.einsum("bcgjls,bcsgjp->bclgjp", cb[:, :, :, None] * lmat, x)
    decay_to_end = jnp.moveaxis(jnp.exp(acs[..., -1:] - acs), -1, 2)
    states = jnp.einsum("bclgn,bclgjp->bcgjpn", bc, x * decay_to_end[..., None])
    chunk_decay = jnp.exp(acs[..., -1])

    def step(h, inp):
        s_c, d_c = inp
        return d_c[..., None, None] * h + s_c, h

    _, prev = lax.scan(step, jnp.zeros_like(states[:, 0]),
                       (jnp.moveaxis(states, 1, 0), jnp.moveaxis(chunk_decay, 1, 0)))
    prev = jnp.moveaxis(prev, 0, 1)
    decay_from_start = jnp.exp(jnp.moveaxis(acs, -1, 2))
    y_off = jnp.einsum("bclgn,bcgjpn->bclgjp", cc, prev) * decay_from_start[..., None]
    return (y_diag + y_off).reshape(b, s, SSD_HEADS, p)


def ssd_branch(z, xbc, dt_raw, conv_w, conv_b, dt_bias, a_log, d_skip, norm_w):
    b, s = z.shape[:2]
    xbc = jax.nn.silu(causal_depthwise_conv(xbc, conv_w, conv_b))
    xs, bm, cm = jnp.split(xbc, (SSD_WIDTH, SSD_WIDTH + SSD_GROUPS * SSD_STATE), axis=-1)
    dt = jax.nn.softplus(dt_raw.astype(jnp.float32) + dt_bias.astype(jnp.float32))
    a = -jnp.exp(a_log.astype(jnp.float32))
    xh = xs.astype(jnp.float32).reshape(b, s, SSD_HEADS, SSD_HEAD_DIM)
    y = ssd_chunked(xh, dt, a,
                    bm.astype(jnp.float32).reshape(b, s, SSD_GROUPS, SSD_STATE),
                    cm.astype(jnp.float32).reshape(b, s, SSD_GROUPS, SSD_STATE))
    y = y + d_skip.astype(jnp.float32)[:, None] * xh
    y = y.reshape(b, s, SSD_WIDTH) * jax.nn.silu(z.astype(jnp.float32))
    yg = y.reshape(b, s, SSD_GROUPS, SSD_WIDTH // SSD_GROUPS)
    yg = yg * lax.rsqrt(jnp.mean(jnp.square(yg), axis=-1, keepdims=True) + RMS_EPS)
    return (yg.reshape(b, s, SSD_WIDTH) * norm_w).astype(z.dtype)


def rglru_branch(gate_in, x_in, conv_w, conv_b, w_a, b_a, w_i, b_i, lam):
    b, s = x_in.shape[:2]
    xr = causal_depthwise_conv(x_in, conv_w, conv_b)
    xb = xr.reshape(b, s, LRU_BLOCKS, LRU_BLOCK_DIM)
    r = jax.nn.sigmoid(jnp.einsum("bskd,kde->bske", xb, w_a).reshape(b, s, LRU_WIDTH) + b_a)
    i = jax.nn.sigmoid(jnp.einsum("bskd,kde->bske", xb, w_i).reshape(b, s, LRU_WIDTH) + b_i)
    log_a = -LRU_C * r.astype(jnp.float32) * jax.nn.softplus(-lam.astype(jnp.float32))
    a = jnp.exp(log_a)
    u = jnp.sqrt(-jnp.expm1(2.0 * log_a)) * (i * xr).astype(jnp.float32)

    def combine(left, right):
        a_l, b_l = left
        a_r, b_r = right
        return a_l * a_r, a_r * b_l + b_r

    _, h = lax.associative_scan(combine, (a, u), axis=1)
    return (jax.nn.gelu(gate_in).astype(jnp.float32) * h).astype(x_in.dtype)


def memory_cross_attention(h, mem, w_q, w_k, w_v, w_o):
    b, s, _ = h.shape
    m = mem.shape[1]
    q = (h @ w_q).reshape(b, s, XATTN_HEADS, XATTN_HEAD_DIM)
    k = (mem @ w_k).reshape(b, m, XATTN_HEADS, XATTN_HEAD_DIM)
    v = (mem @ w_v).reshape(b, m, XATTN_HEADS, XATTN_HEAD_DIM)
    scores = jnp.einsum("bshd,bmhd->bhsm", q, k).astype(jnp.float32) * (XATTN_HEAD_DIM ** -0.5)
    probs = jax.nn.softmax(scores, axis=-1).astype(v.dtype)
    o = jnp.einsum("bhsm,bmhd->bshd", probs, v).reshape(b, s, XATTN_WIDTH)
    return o @ w_o


def peer_ffn(h, w_query, keys_1, keys_2, expert_u, expert_v):
    b, s, d = h.shape
    q = (h @ w_query).reshape(b, s, PEER_HEADS, 2, PEER_KEY_DIM)
    s1 = jnp.einsum("bshd,hkd->bshk", q[..., 0, :], keys_1).astype(jnp.float32)
    s2 = jnp.einsum("bshd,hkd->bshk", q[..., 1, :], keys_2).astype(jnp.float32)
    v1, i1 = lax.top_k(s1, PEER_TOPK)
    v2, i2 = lax.top_k(s2, PEER_TOPK)
    cand_s = (v1[..., :, None] + v2[..., None, :]).reshape(b, s, PEER_HEADS, PEER_TOPK * PEER_TOPK)
    cand_i = (i1[..., :, None] * PEER_N_KEYS + i2[..., None, :]).reshape(b, s, PEER_HEADS, PEER_TOPK * PEER_TOPK)
    top_s, top_pos = lax.top_k(cand_s, PEER_TOPK)
    expert_idx = jnp.take_along_axis(cand_i, top_pos, axis=-1)
    gates = jax.nn.softmax(top_s, axis=-1)
    nb = (b * s) // PEER_TOKEN_BLOCK
    sel = PEER_HEADS * PEER_TOPK
    xt = h.reshape(nb, PEER_TOKEN_BLOCK, d)
    it = expert_idx.reshape(nb, PEER_TOKEN_BLOCK, sel)
    gt = gates.astype(h.dtype).reshape(nb, PEER_TOKEN_BLOCK, sel)

    def block(args):
        xb, ib, gb = args
        u = jnp.take(expert_u, ib, axis=0)
        v = jnp.take(expert_v, ib, axis=0)
        act = jax.nn.gelu(jnp.einsum("td,tkd->tk", xb, u))
        return jnp.einsum("tk,tkd->td", gb * act, v)

    return lax.map(block, (xt, it, gt)).reshape(b, s, d)


def hybrid_layer(h, mem, w_in, ssd_conv_w, ssd_conv_b, ssd_dt_bias, ssd_a_log, ssd_d,
                 ssd_norm_w, lru_conv_w, lru_conv_b, lru_w_a, lru_b_a, lru_w_i, lru_b_i,
                 lru_lambda, w_proj_ssd, w_proj_lru, w_mix_out, ln1_g, ln1_b, xa_w_q,
                 xa_w_k, xa_w_v, xa_w_o, ln2_g, ln2_b, peer_w_q, peer_keys_1, peer_keys_2,
                 peer_u, peer_v, ln3_g, ln3_b):
    proj = h @ w_in
    z, xbc, dt_raw, lru_gate, lru_x, gate_ssd, gate_lru = jnp.split(proj, IN_SPLIT_POINTS, axis=-1)
    y_ssd = ssd_branch(z, xbc, dt_raw, ssd_conv_w, ssd_conv_b, ssd_dt_bias, ssd_a_log,
                       ssd_d, ssd_norm_w)
    y_lru = rglru_branch(lru_gate, lru_x, lru_conv_w, lru_conv_b, lru_w_a, lru_b_a,
                         lru_w_i, lru_b_i, lru_lambda)
    merged = (jax.nn.sigmoid(gate_ssd) * (y_ssd @ w_proj_ssd)
              + jax.nn.sigmoid(gate_lru) * (y_lru @ w_proj_lru))
    h = layer_norm(DEEPNORM_ALPHA * h + merged @ w_mix_out, ln1_g, ln1_b)
    h = layer_norm(DEEPNORM_ALPHA * h + memory_cross_attention(h, mem, xa_w_q, xa_w_k, xa_w_v, xa_w_o),
                   ln2_g, ln2_b)
    h = layer_norm(DEEPNORM_ALPHA * h + peer_ffn(h, peer_w_q, peer_keys_1, peer_keys_2, peer_u, peer_v),
                   ln3_g, ln3_b)
    return h


def _normal(k, shape, scale):
    return jax.random.normal(k, shape, jnp.float32) * scale


def setup_inputs(seed: int = 0) -> dict:
    key = jax.random.key(seed)
    ks = iter(jax.random.split(key, 48))
    L, D = DEPTH, D_MODEL
    beta = DEEPNORM_BETA
    dt_start = IN_Z + IN_XBC
    col_scale = jnp.ones((IN_COLS,), jnp.float32).at[dt_start:dt_start + IN_DT].set(0.1)
    dt0 = jnp.exp(jax.random.uniform(next(ks), (L, SSD_HEADS), jnp.float32,
                                     math.log(SSD_DT_MIN), math.log(SSD_DT_MAX)))
    lam_u = jax.random.uniform(next(ks), (L, LRU_WIDTH), jnp.float32, 0.9, 0.999)
    lam_a = lam_u ** (1.0 / LRU_C)
    inp = {}
    inp["x"] = _normal(next(ks), (BATCH, SEQ, D), 1.0)
    inp["mem"] = _normal(next(ks), (BATCH, MEM_LEN, D), 1.0)
    inp["ln_in_g"] = 1.0 + _normal(next(ks), (D,), 0.02)
    inp["ln_in_b"] = _normal(next(ks), (D,), 0.02)
    inp["w_in"] = _normal(next(ks), (L, D, IN_COLS), D ** -0.5) * col_scale
    inp["ssd_conv_w"] = _normal(next(ks), (L, SSD_CONV, SSD_CONV_DIM), SSD_CONV ** -0.5)
    inp["ssd_conv_b"] = _normal(next(ks), (L, SSD_CONV_DIM), 0.02)
    inp["ssd_dt_bias"] = dt0 + jnp.log(-jnp.expm1(-dt0))
    inp["ssd_a_log"] = jnp.log(jax.random.uniform(next(ks), (L, SSD_HEADS), jnp.float32, 1.0, 16.0))
    inp["ssd_d"] = 1.0 + _normal(next(ks), (L, SSD_HEADS), 0.02)
    inp["ssd_norm_w"] = 1.0 + _normal(next(ks), (L, SSD_WIDTH), 0.02)
    inp["lru_conv_w"] = _normal(next(ks), (L, LRU_CONV, LRU_WIDTH), LRU_CONV ** -0.5)
    inp["lru_conv_b"] = _normal(next(ks), (L, LRU_WIDTH), 0.02)
    inp["lru_w_a"] = _normal(next(ks), (L, LRU_BLOCKS, LRU_BLOCK_DIM, LRU_BLOCK_DIM), LRU_BLOCK_DIM ** -0.5)
    inp["lru_b_a"] = _normal(next(ks), (L, LRU_WIDTH), 0.02)
    inp["lru_w_i"] = _normal(next(ks), (L, LRU_BLOCKS, LRU_BLOCK_DIM, LRU_BLOCK_DIM), LRU_BLOCK_DIM ** -0.5)
    inp["lru_b_i"] = _normal(next(ks), (L, LRU_WIDTH), 0.02)
    inp["lru_lambda"] = jnp.log(lam_a) - jnp.log1p(-lam_a)
    inp["w_proj_ssd"] = _normal(next(ks), (L, SSD_WIDTH, D), SSD_WIDTH ** -0.5)
    inp["w_proj_lru"] = _normal(next(ks), (L, LRU_WIDTH, D), LRU_WIDTH ** -0.5)
    inp["w_mix_out"] = _normal(next(ks), (L, D, D), beta * D ** -0.5)
    inp["ln1_g"] = 1.0 + _normal(next(ks), (L, D), 0.02)
    inp["ln1_b"] = _normal(next(ks), (L, D), 0.02)
    inp["xa_w_q"] = _normal(next(ks), (L, D, XATTN_WIDTH), D ** -0.5)
    inp["xa_w_k"] = _normal(next(ks), (L, D, XATTN_WIDTH), D ** -0.5)
    inp["xa_w_v"] = _normal(next(ks), (L, D, XATTN_WIDTH), beta * D ** -0.5)
    inp["xa_w_o"] = _normal(next(ks), (L, XATTN_WIDTH, D), beta * XATTN_WIDTH ** -0.5)
    inp["ln2_g"] = 1.0 + _normal(next(ks), (L, D), 0.02)
    inp["ln2_b"] = _normal(next(ks), (L, D), 0.02)
    inp["peer_w_q"] = _normal(next(ks), (L, D, PEER_HEADS * 2 * PEER_KEY_DIM), D ** -0.5)
    inp["peer_keys_1"] = _normal(next(ks), (L, PEER_HEADS, PEER_N_KEYS, PEER_KEY_DIM), PEER_KEY_DIM ** -0.5)
    inp["peer_keys_2"] = _normal(next(ks), (L, PEER_HEADS, PEER_N_KEYS, PEER_KEY_DIM), PEER_KEY_DIM ** -0.5)
    inp["peer_u"] = _normal(next(ks), (L, PEER_N_EXPERTS, D), D ** -0.5)
    inp["peer_v"] = _normal(next(ks), (L, PEER_N_EXPERTS, D), beta)
    inp["ln3_g"] = 1.0 + _normal(next(ks), (L, D), 0.02)
    inp["ln3_b"] = _normal(next(ks), (L, D), 0.02)
    return inp


def reference(x, mem, ln_in_g, ln_in_b, w_in, ssd_conv_w, ssd_conv_b, ssd_dt_bias, ssd_a_log,
              ssd_d, ssd_norm_w, lru_conv_w, lru_conv_b, lru_w_a, lru_b_a, lru_w_i, lru_b_i,
              lru_lambda, w_proj_ssd, w_proj_lru, w_mix_out, ln1_g, ln1_b, xa_w_q, xa_w_k,
              xa_w_v, xa_w_o, ln2_g, ln2_b, peer_w_q, peer_keys_1, peer_keys_2, peer_u,
              peer_v, ln3_g, ln3_b):
    h = layer_norm(x, ln_in_g, ln_in_b)
    for l in range(DEPTH):
        h = hybrid_layer(h, mem, w_in[l], ssd_conv_w[l], ssd_conv_b[l], ssd_dt_bias[l],
                         ssd_a_log[l], ssd_d[l], ssd_norm_w[l], lru_conv_w[l], lru_conv_b[l],
                         lru_w_a[l], lru_b_a[l], lru_w_i[l], lru_b_i[l], lru_lambda[l],
                         w_proj_ssd[l], w_proj_lru[l], w_mix_out[l], ln1_g[l], ln1_b[l],
                         xa_w_q[l], xa_w_k[l], xa_w_v[l], xa_w_o[l], ln2_g[l], ln2_b[l],
                         peer_w_q[l], peer_keys_1[l], peer_keys_2[l], peer_u[l], peer_v[l],
                         ln3_g[l], ln3_b[l])
    return h
```

```python
import functools

import jax
import jax.numpy as jnp
from jax import lax
from jax.experimental import pallas as pl
from jax.experimental.pallas import tpu as pltpu

F32 = jnp.float32
BF16 = jnp.bfloat16

D_MODEL = 4096
SSD_HEADS = 64
SSD_HEAD_DIM = 64
SSD_GROUPS = 8
SSD_STATE = 128
SSD_CHUNK = 128
SSD_GROUP_WIDTH = D_MODEL // SSD_GROUPS
LRU_BLOCKS = 16
LRU_BLOCK_DIM = 256
LRU_C = 8.0
XATTN_HEADS = 4
XATTN_HEAD_DIM = 128
PEER_HEADS = 8
PEER_N_KEYS = 128
PEER_TOPK = 16
DEEPNORM_ALPHA = 2.0 ** 0.25
LN_EPS = 1e-5
RMS_EPS = 1e-6
CONV_TAPS = 4
HIST_ROWS = 8

VMEM_LIMIT_BYTES = 58 * 1024 * 1024

COL_Z, COL_XS, COL_LRU_GATE, COL_LRU_X, COL_GATE_SSD, COL_GATE_LRU = (
    0, 4096, 8192, 12288, 16384, 20480)
COL_B, COL_C = 24576, 25600
PROJ_COLS = 26624


def _params(sem):
  return pltpu.CompilerParams(dimension_semantics=sem, vmem_limit_bytes=VMEM_LIMIT_BYTES)


def _layer_norm(x, g, b):
  mu = jnp.mean(x, axis=-1, keepdims=True)
  xc = x - mu
  var = jnp.mean(xc * xc, axis=-1, keepdims=True)
  return xc * lax.rsqrt(var + LN_EPS) * g + b


def _dot(a, b):
  return jnp.dot(a, b, preferred_element_type=F32)


def _dot_nt(a, b):
  return lax.dot_general(a, b, (((1,), (1,)), ((), ())), preferred_element_type=F32)


def _ln_kernel(x_ref, g_ref, b_ref, o_ref, ob_ref):
  y = _layer_norm(x_ref[...], g_ref[...], b_ref[...])
  o_ref[...] = y
  ob_ref[...] = y.astype(BF16)


def _res_ln_kernel(h_ref, y_ref, g_ref, b_ref, o_ref, ob_ref):
  y = _layer_norm(DEEPNORM_ALPHA * h_ref[...] + y_ref[...], g_ref[...], b_ref[...])
  o_ref[...] = y
  ob_ref[...] = y.astype(BF16)


def _row_spec(bm, d):
  return pl.BlockSpec((bm, d), lambda i: (i, 0))


def _vec_spec(d):
  return pl.BlockSpec((1, d), lambda i: (0, 0))


def _ln_call(x, g, b, bm=256):
  t, d = x.shape
  return pl.pallas_call(
      _ln_kernel,
      grid=(t // bm,),
      in_specs=[_row_spec(bm, d), _vec_spec(d), _vec_spec(d)],
      out_specs=[_row_spec(bm, d), _row_spec(bm, d)],
      out_shape=[jax.ShapeDtypeStruct((t, d), F32), jax.ShapeDtypeStruct((t, d), BF16)],
      compiler_params=_params(("arbitrary",)),
      name="ln_in",
  )(x, g.reshape(1, d), b.reshape(1, d))


def _res_ln_call(h, y, g, b, name, bm=256):
  t, d = h.shape
  return pl.pallas_call(
      _res_ln_kernel,
      grid=(t // bm,),
      in_specs=[_row_spec(bm, d), _row_spec(bm, d), _vec_spec(d), _vec_spec(d)],
      out_specs=[_row_spec(bm, d), _row_spec(bm, d)],
      out_shape=[jax.ShapeDtypeStruct((t, d), F32), jax.ShapeDtypeStruct((t, d), BF16)],
      compiler_params=_params(("arbitrary",)),
      name=name,
  )(h, y, g.reshape(1, d), b.reshape(1, d))


def _mm_kernel(x_ref, w_ref, o_ref):
  o_ref[...] = _dot(x_ref[...], w_ref[...]).astype(o_ref.dtype)


def _matmul(x, w, out_dtype, name, bm=1024, bn=1024):
  m, k = x.shape
  n = w.shape[1]
  bm, bn = min(bm, m), min(bn, n)
  return pl.pallas_call(
      _mm_kernel,
      grid=(m // bm, n // bn),
      in_specs=[pl.BlockSpec((bm, k), lambda i, j: (i, 0)),
                pl.BlockSpec((k, bn), lambda i, j: (0, j))],
      out_specs=pl.BlockSpec((bm, bn), lambda i, j: (i, j)),
      out_shape=jax.ShapeDtypeStruct((m, n), out_dtype),
      compiler_params=_params(("arbitrary", "arbitrary")),
      name=name,
  )(x, w)


def _causal_conv(e_ref, raw, w_ref, b_ref, rows):
  e_ref[HIST_ROWS:HIST_ROWS + rows, :] = raw.astype(F32)
  y = b_ref[...]
  for k in range(CONV_TAPS):
    off = HIST_ROWS - (CONV_TAPS - 1) + k
    y = y + w_ref[k:k + 1, :] * e_ref[off:off + rows, :]
  e_ref[0:HIST_ROWS, :] = e_ref[rows:rows + HIST_ROWS, :]
  return y


def _softplus(x):
  return jnp.maximum(x, 0.0) + jnp.log1p(jnp.exp(-jnp.abs(x)))


def _ssd_kernel(z_ref, x_ref, b_ref, c_ref, dt_ref, cwx_ref, cbx_ref, cwb_ref, cbb_ref,
                cwc_ref, cbc_ref, dtb_ref, alog_ref, dsk_ref, nw_ref, o_ref,
                xe_ref, be_ref, ce_ref, st_ref):
  L = x_ref.shape[0]
  n = SSD_STATE
  pair_w = 2 * SSD_HEAD_DIM

  @pl.when(pl.program_id(2) == 0)
  def _init():
    xe_ref[0:HIST_ROWS, :] = jnp.zeros((HIST_ROWS, xe_ref.shape[1]), F32)
    be_ref[0:HIST_ROWS, :] = jnp.zeros((HIST_ROWS, be_ref.shape[1]), F32)
    ce_ref[0:HIST_ROWS, :] = jnp.zeros((HIST_ROWS, ce_ref.shape[1]), F32)
    st_ref[...] = jnp.zeros(st_ref.shape, F32)

  def conv_silu(e_ref, raw_ref, w_ref, bias_ref):
    y = _causal_conv(e_ref, raw_ref[...], w_ref, bias_ref, L)
    return y * jax.nn.sigmoid(y)

  xs = conv_silu(xe_ref, x_ref, cwx_ref, cbx_ref)
  bm = conv_silu(be_ref, b_ref, cwb_ref, cbb_ref)
  cm = conv_silu(ce_ref, c_ref, cwc_ref, cbc_ref)

  dt_t = _softplus(dt_ref[...] + dtb_ref[...])
  a_neg = -jnp.exp(alog_ref[...])
  acs_t = dt_t * a_neg
  lane_t = lax.broadcasted_iota(jnp.int32, acs_t.shape, 1)
  shift = 1
  while shift < L:
    acs_t = acs_t + jnp.where(lane_t >= shift, pltpu.roll(acs_t, shift, axis=1), 0.0)
    shift *= 2
  a_last = acs_t[:, L - 1:L]
  w_t = dt_t * jnp.exp(a_last - acs_t)
  acs = jnp.concatenate([acs_t, jnp.zeros((L - 8, L), F32)], axis=0).T

  bm16 = bm.astype(BF16)
  cm16 = cm.astype(BF16)
  cb = _dot_nt(cm16, bm16)
  b_t = bm.T
  row = lax.broadcasted_iota(jnp.int32, (L, L), 0)
  col = lax.broadcasted_iota(jnp.int32, (L, L), 1)
  causal = row >= col
  lo = lax.broadcasted_iota(jnp.int32, (L, pair_w), 1) < SSD_HEAD_DIM
  lo_n = lax.broadcasted_iota(jnp.int32, (n, pair_w), 1) < SSD_HEAD_DIM

  st = st_ref[...]
  y_off = _dot(cm16, st.astype(BF16))

  ys = []
  for q in range(SSD_GROUP_WIDTH // pair_w):
    sl = slice(q * pair_w, (q + 1) * pair_w)
    xq = xs[:, sl]
    rhs = jnp.concatenate([jnp.where(lo, xq, 0.0), jnp.where(lo, 0.0, xq)],
                          axis=0).astype(BF16)
    m_parts, bw_parts = [], []
    for j in (2 * q, 2 * q + 1):
      seg = acs[:, j:j + 1] - acs_t[j:j + 1, :]
      lm = jnp.exp(jnp.where(causal, seg, -jnp.inf))
      m_parts.append((cb * lm * dt_t[j:j + 1, :]).astype(BF16))
      bw_parts.append((b_t * w_t[j:j + 1, :]).astype(BF16))
    lhs = jnp.concatenate([jnp.concatenate(m_parts, axis=1),
                           jnp.concatenate(bw_parts, axis=1)], axis=0)
    res = _dot(lhs, rhs)
    dfs = jnp.where(lo, jnp.exp(acs[:, 2 * q:2 * q + 1]), jnp.exp(acs[:, 2 * q + 1:2 * q + 2]))
    ys.append(res[:L] + y_off[:, sl] * dfs)
    chunk_decay = jnp.where(lo_n, jnp.exp(a_last[2 * q:2 * q + 1, :]),
                            jnp.exp(a_last[2 * q + 1:2 * q + 2, :]))
    st_ref[:, sl] = st[:, sl] * chunk_decay + res[L:]

  y = jnp.concatenate(ys, axis=1) + dsk_ref[...] * xs
  zf = z_ref[...].astype(F32)
  y = y * (zf * jax.nn.sigmoid(zf))
  ms = jnp.mean(y * y, axis=-1, keepdims=True)
  o_ref[...] = (y * lax.rsqrt(ms + RMS_EPS) * nw_ref[...]).astype(BF16)


def _ssd_call(proj, dt_t, conv_w, conv_b, dt_bias, a_log, d_skip, norm_w, batch, seq):
  t = proj.shape[0]
  L, gw, n = SSD_CHUNK, SSD_GROUP_WIDTH, SSD_STATE
  nc = seq // L
  hpg = SSD_HEADS // SSD_GROUPS
  cw_x, cw_b, cw_c = conv_w[:, :D_MODEL], conv_w[:, D_MODEL:D_MODEL + 1024], conv_w[:, D_MODEL + 1024:]
  cb_x = conv_b[:D_MODEL].reshape(1, -1)
  cb_b = conv_b[D_MODEL:D_MODEL + 1024].reshape(1, -1)
  cb_c = conv_b[D_MODEL + 1024:].reshape(1, -1)
  d_cols = jnp.repeat(d_skip, SSD_HEAD_DIM).reshape(1, D_MODEL)

  def rows(b, g, c):
    return b * nc + c

  in_specs = [
      pl.BlockSpec((L, gw), lambda b, g, c: (rows(b, g, c), COL_Z // gw + g)),
      pl.BlockSpec((L, gw), lambda b, g, c: (rows(b, g, c), COL_XS // gw + g)),
      pl.BlockSpec((L, n), lambda b, g, c: (rows(b, g, c), COL_B // n + g)),
      pl.BlockSpec((L, n), lambda b, g, c: (rows(b, g, c), COL_C // n + g)),
      pl.BlockSpec((hpg, L), lambda b, g, c: (g, rows(b, g, c))),
      pl.BlockSpec((CONV_TAPS, gw), lambda b, g, c: (0, g)),
      pl.BlockSpec((1, gw), lambda b, g, c: (0, g)),
      pl.BlockSpec((CONV_TAPS, n), lambda b, g, c: (0, g)),
      pl.BlockSpec((1, n), lambda b, g, c: (0, g)),
      pl.BlockSpec((CONV_TAPS, n), lambda b, g, c: (0, g)),
      pl.BlockSpec((1, n), lambda b, g, c: (0, g)),
      pl.BlockSpec((hpg, 1), lambda b, g, c: (g, 0)),
      pl.BlockSpec((hpg, 1), lambda b, g, c: (g, 0)),
      pl.BlockSpec((1, gw), lambda b, g, c: (0, g)),
      pl.BlockSpec((1, gw), lambda b, g, c: (0, g)),
  ]
  return pl.pallas_call(
      _ssd_kernel,
      grid=(batch, SSD_GROUPS, nc),
      in_specs=in_specs,
      out_specs=pl.BlockSpec((L, gw), lambda b, g, c: (rows(b, g, c), g)),
      out_shape=jax.ShapeDtypeStruct((t, D_MODEL), BF16),
      scratch_shapes=[
          pltpu.VMEM((L + HIST_ROWS, gw), F32),
          pltpu.VMEM((L + HIST_ROWS, n), F32),
          pltpu.VMEM((L + HIST_ROWS, n), F32),
          pltpu.VMEM((n, gw), F32),
      ],
      compiler_params=_params(("arbitrary", "arbitrary", "arbitrary")),
      name="ssd_scan",
  )(proj, proj, proj, proj, dt_t, cw_x, cb_x, cw_b, cb_b, cw_c, cb_c,
    dt_bias.reshape(-1, 1), a_log.reshape(-1, 1), d_cols, norm_w.reshape(1, -1))


def _lru_kernel(x_ref, g_ref, cw_ref, cb_ref, wa_ref, ba_ref, wi_ref, bi_ref, lam_ref,
                o_ref, xe_ref, h_ref):
  ts = x_ref.shape[0]

  @pl.when(pl.program_id(2) == 0)
  def _init():
    xe_ref[0:HIST_ROWS, :] = jnp.zeros((HIST_ROWS, xe_ref.shape[1]), F32)
    h_ref[...] = jnp.zeros(h_ref.shape, F32)

  xr = _causal_conv(xe_ref, x_ref[...], cw_ref, cb_ref, ts)
  xr16 = xr.astype(BF16)
  r = jax.nn.sigmoid(_dot(xr16, wa_ref[...]) + ba_ref[...])
  i = jax.nn.sigmoid(_dot(xr16, wi_ref[...]) + bi_ref[...])
  log_a = (-LRU_C) * r * _softplus(-lam_ref[...])
  a = jnp.exp(log_a)
  th = jnp.tanh(log_a)
  u = jnp.sqrt(-2.0 * th / (1.0 - th)) * (i * xr)

  row = lax.broadcasted_iota(jnp.int32, a.shape, 0)
  shift = 1
  while shift < ts:
    keep = row >= shift
    a_prev = jnp.where(keep, pltpu.roll(a, shift, axis=0), 1.0)
    u_prev = jnp.where(keep, pltpu.roll(u, shift, axis=0), 0.0)
    u = a * u_prev + u
    a = a * a_prev
    shift *= 2
  h = a * h_ref[0:1, :] + u
  h_ref[...] = jnp.broadcast_to(h[ts - 1:ts, :], h_ref.shape)
  o_ref[...] = (jax.nn.gelu(g_ref[...].astype(F32)) * h).astype(BF16)


def _lru_call(proj, conv_w, conv_b, w_a, b_a, w_i, b_i, lam, batch, seq, ts=512):
  t = proj.shape[0]
  bd = LRU_BLOCK_DIM
  nt = seq // ts

  def rows(b, k, s):
    return b * nt + s

  vec = lambda: pl.BlockSpec((1, bd), lambda b, k, s: (0, k))
  in_specs = [
      pl.BlockSpec((ts, bd), lambda b, k, s: (rows(b, k, s), COL_LRU_X // bd + k)),
      pl.BlockSpec((ts, bd), lambda b, k, s: (rows(b, k, s), COL_LRU_GATE // bd + k)),
      pl.BlockSpec((CONV_TAPS, bd), lambda b, k, s: (0, k)),
      vec(),
      pl.BlockSpec((None, bd, bd), lambda b, k, s: (k, 0, 0)),
      vec(),
      pl.BlockSpec((None, bd, bd), lambda b, k, s: (k, 0, 0)),
      vec(),
      vec(),
  ]
  return pl.pallas_call(
      _lru_kernel,
      grid=(batch, LRU_BLOCKS, nt),
      in_specs=in_specs,
      out_specs=pl.BlockSpec((ts, bd), lambda b, k, s: (rows(b, k, s), k)),
      out_shape=jax.ShapeDtypeStruct((t, D_MODEL), BF16),
      scratch_shapes=[pltpu.VMEM((ts + HIST_ROWS, bd), F32), pltpu.VMEM((HIST_ROWS, bd), F32)],
      compiler_params=_params(("arbitrary", "arbitrary", "arbitrary")),
      name="rglru_scan",
  )(proj, proj, conv_w, conv_b.reshape(1, -1), w_a.astype(BF16), b_a.reshape(1, -1),
    w_i.astype(BF16), b_i.reshape(1, -1), lam.reshape(1, -1))


def _merge_kernel(ys_ref, yl_ref, ws_ref, wl_ref, ga_ref, gb_ref, o_ref):
  pa = _dot(ys_ref[...], ws_ref[...])
  pb = _dot(yl_ref[...], wl_ref[...])
  ga = jax.nn.sigmoid(ga_ref[...].astype(F32))
  gb = jax.nn.sigmoid(gb_ref[...].astype(F32))
  o_ref[...] = (ga * pa + gb * pb).astype(o_ref.dtype)


def _merge_call(y_ssd, y_lru, w_ssd, w_lru, proj, bm=512, bn=512):
  t, d = y_ssd.shape
  return pl.pallas_call(
      _merge_kernel,
      grid=(t // bm, d // bn),
      in_specs=[
          pl.BlockSpec((bm, d), lambda i, j: (i, 0)),
          pl.BlockSpec((bm, d), lambda i, j: (i, 0)),
          pl.BlockSpec((d, bn), lambda i, j: (0, j)),
          pl.BlockSpec((d, bn), lambda i, j: (0, j)),
          pl.BlockSpec((bm, bn), lambda i, j: (i, COL_GATE_SSD // bn + j)),
          pl.BlockSpec((bm, bn), lambda i, j: (i, COL_GATE_LRU // bn + j)),
      ],
      out_specs=pl.BlockSpec((bm, bn), lambda i, j: (i, j)),
      out_shape=jax.ShapeDtypeStruct((t, d), BF16),
      compiler_params=_params(("arbitrary", "arbitrary")),
      name="gated_merge",
  )(y_ssd, y_lru, w_ssd, w_lru, proj, proj)


def _xattn_kernel(h_ref, hb_ref, wq_ref, k_ref, v_ref, wo_ref, g_ref, b_ref, o_ref, ob_ref):
  hd = XATTN_HEAD_DIM
  q = _dot(hb_ref[...], wq_ref[...]).astype(BF16)
  k = k_ref[...]
  v = v_ref[...]
  outs = []
  for head in range(XATTN_HEADS):
    sl = slice(head * hd, (head + 1) * hd)
    s = _dot_nt(q[:, sl], k[:, sl]) * (hd ** -0.5)
    s = s - jnp.max(s, axis=-1, keepdims=True)
    p = jnp.exp(s)
    p = p / jnp.sum(p, axis=-1, keepdims=True)
    outs.append(_dot(p.astype(BF16), v[:, sl]))
  o = jnp.concatenate(outs, axis=1).astype(BF16)
  y = _dot(o, wo_ref[...])
  out = _layer_norm(DEEPNORM_ALPHA * h_ref[...] + y, g_ref[...], b_ref[...])
  o_ref[...] = out
  ob_ref[...] = out.astype(BF16)


def _xattn_call(h, hb, w_q, k, v, w_o, g, b, seq, tm=256):
  t, d = h.shape
  xw = w_q.shape[1]
  mem_len = k.shape[1]
  per_batch = seq // tm
  return pl.pallas_call(
      _xattn_kernel,
      grid=(t // tm,),
      in_specs=[
          _row_spec(tm, d), _row_spec(tm, d),
          pl.BlockSpec((d, xw), lambda i: (0, 0)),
          pl.BlockSpec((None, mem_len, xw), lambda i: (i // per_batch, 0, 0)),
          pl.BlockSpec((None, mem_len, xw), lambda i: (i // per_batch, 0, 0)),
          pl.BlockSpec((xw, d), lambda i: (0, 0)),
          _vec_spec(d), _vec_spec(d),
      ],
      out_specs=[_row_spec(tm, d), _row_spec(tm, d)],
      out_shape=[jax.ShapeDtypeStruct((t, d), F32), jax.ShapeDtypeStruct((t, d), BF16)],
      compiler_params=_params(("arbitrary",)),
      name="mem_xattn_ln2",
  )(h, hb, w_q, k, v, w_o, g.reshape(1, d), b.reshape(1, d))


_PEER_CAND_PAIRS = tuple((a, b) for a in range(PEER_TOPK) for b in range(PEER_TOPK)
                         if (a + 1) * (b + 1) <= PEER_TOPK)
_PEER_CAND_ROWS = 56


def _peer_route_kernel(q_ref, k1_ref, k2_ref, s1_ref, s2_ref, st_ref, cand_ref):
  tm = q_ref.shape[0]
  kd = PEER_N_KEYS
  neg_inf = -jnp.inf

  def top_values(s):
    vals = []
    cur = s
    for _ in range(PEER_TOPK):
      m = jnp.max(cur, axis=0, keepdims=True)
      vals.append(m)
      cur = jnp.where(cur == m, neg_inf, cur)
    return vals

  cand_ref[...] = jnp.full(cand_ref.shape, neg_inf, F32)
  for head in range(PEER_HEADS):
    q1 = q_ref[:, head * 2 * kd:head * 2 * kd + kd]
    q2 = q_ref[:, head * 2 * kd + kd:(head + 1) * 2 * kd]
    s1 = _dot_nt(k1_ref[head], q1)
    s2 = _dot_nt(k2_ref[head], q2)
    s1_ref[head] = s1
    s2_ref[head] = s2
    v1 = top_values(s1)
    v2 = top_values(s2)
    for idx, (a, b) in enumerate(_PEER_CAND_PAIRS):
      cand_ref[idx:idx + 1, :] = v1[a] + v2[b]
    cand = cand_ref[...]
    cur = cand
    count = jnp.zeros((1, tm), F32)
    tau = jnp.full((1, tm), neg_inf, F32)
    for _ in range(PEER_TOPK):
      m = jnp.max(cur, axis=0, keepdims=True)
      eq = cur == m
      new_count = count + jnp.sum(jnp.where(eq, 1.0, 0.0), axis=0, keepdims=True)
      tau = jnp.where(count < PEER_TOPK, jnp.where(new_count >= PEER_TOPK, m, tau), tau)
      count = new_count
      cur = jnp.where(eq, neg_inf, cur)
    top = v1[0] + v2[0]
    z = jnp.sum(jnp.where(cand >= tau, jnp.exp(cand - top), 0.0), axis=0, keepdims=True)
    st_ref[0, head:head + 1, :] = tau
    st_ref[1, head:head + 1, :] = v1[0]
    st_ref[2, head:head + 1, :] = v2[0]
    st_ref[3, head:head + 1, :] = 1.0 / z


def _peer_route_call(q, keys_1, keys_2, tm=256):
  t = q.shape[0]
  h, kd = PEER_HEADS, PEER_N_KEYS
  score_spec = pl.BlockSpec((h, kd, tm), lambda i: (0, 0, i))
  return pl.pallas_call(
      _peer_route_kernel,
      grid=(t // tm,),
      in_specs=[pl.BlockSpec((tm, q.shape[1]), lambda i: (i, 0)),
                pl.BlockSpec((h, kd, kd), lambda i: (0, 0, 0)),
                pl.BlockSpec((h, kd, kd), lambda i: (0, 0, 0))],
      out_specs=[score_spec, score_spec, pl.BlockSpec((4, h, tm), lambda i: (0, 0, i))],
      out_shape=[jax.ShapeDtypeStruct((h, kd, t), F32), jax.ShapeDtypeStruct((h, kd, t), F32),
                 jax.ShapeDtypeStruct((4, h, t), F32)],
      scratch_shapes=[pltpu.VMEM((_PEER_CAND_ROWS, tm), F32)],
      compiler_params=_params(("arbitrary",)),
      name="peer_route",
  )(q, keys_1, keys_2)


def _peer_dense_kernel(x_ref, u_ref, v_ref, s1_ref, s2_ref, st_ref, o_ref, e2_ref):
  j = pl.program_id(1)
  te = u_ref.shape[0]
  kd = PEER_N_KEYS
  rows_per_tile = te // kd

  @pl.when(j == 0)
  def _init():
    o_ref[...] = jnp.zeros(o_ref.shape, F32)
    for head in range(PEER_HEADS):
      e2_ref[head] = jnp.exp(s2_ref[head] - st_ref[2, head:head + 1, :])

  act_t = jax.nn.gelu(_dot_nt(u_ref[...], x_ref[...]))
  parts = []
  for r in range(rows_per_tile):
    i1 = j * rows_per_tile + r
    g = None
    for head in range(PEER_HEADS):
      s1_row = s1_ref[head, pl.ds(i1, 1), :]
      e1_row = jnp.exp(s1_row - st_ref[1, head:head + 1, :]) * st_ref[3, head:head + 1, :]
      sel = (s2_ref[head] + s1_row) >= st_ref[0, head:head + 1, :]
      term = jnp.where(sel, e2_ref[head] * e1_row, 0.0)
      g = term if g is None else g + term
    parts.append(g * act_t[r * kd:(r + 1) * kd, :])
  p = jnp.concatenate(parts, axis=0).T.astype(BF16)
  o_ref[...] += _dot(p, v_ref[...])


def _peer_dense_call(xb, u, v, s1, s2, stats, tm=512, te=512):
  t, d = xb.shape
  e = u.shape[0]
  h, kd = PEER_HEADS, PEER_N_KEYS
  score_spec = pl.BlockSpec((h, kd, tm), lambda i, j: (0, 0, i))
  return pl.pallas_call(
      _peer_dense_kernel,
      grid=(t // tm, e // te),
      in_specs=[pl.BlockSpec((tm, d), lambda i, j: (i, 0)),
                pl.BlockSpec((te, d), lambda i, j: (j, 0)),
                pl.BlockSpec((te, d), lambda i, j: (j, 0)),
                score_spec, score_spec,
                pl.BlockSpec((4, h, tm), lambda i, j: (0, 0, i))],
      out_specs=pl.BlockSpec((tm, d), lambda i, j: (i, 0)),
      out_shape=jax.ShapeDtypeStruct((t, d), F32),
      scratch_shapes=[pltpu.VMEM((h, kd, tm), F32)],
      compiler_params=_params(("arbitrary", "arbitrary")),
      name="peer_dense",
  )(xb, u, v, s1, s2, stats)


def kernel(x, mem, ln_in_g, ln_in_b, w_in, ssd_conv_w, ssd_conv_b, ssd_dt_bias, ssd_a_log, ssd_d, ssd_norm_w, lru_conv_w, lru_conv_b, lru_w_a, lru_b_a, lru_w_i, lru_b_i, lru_lambda, w_proj_ssd, w_proj_lru, w_mix_out, ln1_g, ln1_b, xa_w_q, xa_w_k, xa_w_v, xa_w_o, ln2_g, ln2_b, peer_w_q, peer_keys_1, peer_keys_2, peer_u, peer_v, ln3_g, ln3_b):
  batch, seq, d = x.shape
  t = batch * seq
  assert w_in.shape[0] == 1, "single-layer trunk"

  w = w_in[0]
  c0 = D_MODEL
  c_xs, c_b, c_c = c0, c0 + D_MODEL, c0 + D_MODEL + 1024
  c_dt = c0 + D_MODEL + 2048
  c_lg = c_dt + SSD_HEADS
  c_lx, c_ga, c_gb = c_lg + D_MODEL, c_lg + 2 * D_MODEL, c_lg + 3 * D_MODEL
  w_main = jnp.concatenate(
      [w[:, :c0], w[:, c_xs:c_b], w[:, c_lg:c_lx], w[:, c_lx:c_ga], w[:, c_ga:c_gb],
       w[:, c_gb:], w[:, c_b:c_c], w[:, c_c:c_dt]], axis=1).astype(BF16)
  w_dt = jnp.pad(w[:, c_dt:c_lg], ((0, 0), (0, 128 - SSD_HEADS))).astype(BF16)

  h0, h0b = _ln_call(x.reshape(t, d), ln_in_g, ln_in_b)
  proj = _matmul(h0b, w_main, BF16, "in_proj")
  dt_raw = _matmul(h0b, w_dt, F32, "dt_proj")
  dt_t = dt_raw[:, :SSD_HEADS].T

  y_ssd = _ssd_call(proj, dt_t, ssd_conv_w[0], ssd_conv_b[0], ssd_dt_bias[0], ssd_a_log[0],
                    ssd_d[0], ssd_norm_w[0], batch, seq)
  y_lru = _lru_call(proj, lru_conv_w[0], lru_conv_b[0], lru_w_a[0], lru_b_a[0], lru_w_i[0],
                    lru_b_i[0], lru_lambda[0], batch, seq)
  merged = _merge_call(y_ssd, y_lru, w_proj_ssd[0].astype(BF16), w_proj_lru[0].astype(BF16), proj)
  mixed = _matmul(merged, w_mix_out[0].astype(BF16), F32, "mix_out")
  h1, h1b = _res_ln_call(h0, mixed, ln1_g[0], ln1_b[0], "res_ln1")

  mem_len = mem.shape[1]
  w_kv = jnp.concatenate([xa_w_k[0], xa_w_v[0]], axis=1).astype(BF16)
  kv = _matmul(mem.reshape(batch * mem_len, d).astype(BF16), w_kv, BF16, "mem_kv")
  xw = xa_w_k.shape[-1]
  k = kv[:, :xw].reshape(batch, mem_len, xw)
  v = kv[:, xw:].reshape(batch, mem_len, xw)
  h2, h2b = _xattn_call(h1, h1b, xa_w_q[0].astype(BF16), k, v, xa_w_o[0].astype(BF16),
                        ln2_g[0], ln2_b[0], seq)

  q = _matmul(h2b, peer_w_q[0].astype(BF16), BF16, "peer_query")
  s1, s2, stats = _peer_route_call(q, peer_keys_1[0].astype(BF16), peer_keys_2[0].astype(BF16))
  ffn = _peer_dense_call(h2b, peer_u[0].astype(BF16), peer_v[0].astype(BF16), s1, s2, stats)
  h3, _ = _res_ln_call(h2, ffn, ln3_g[0], ln3_b[0], "res_ln3")
  return h3.reshape(batch, seq, d)
```

```python
import functools

import jax
import jax.numpy as jnp
from jax import lax
from jax.experimental import pallas as pl
from jax.experimental.pallas import tpu as pltpu

F32 = jnp.float32
BF16 = jnp.bfloat16

D_MODEL = 4096
SSD_HEADS = 64
SSD_HEAD_DIM = 64
SSD_GROUPS = 8
SSD_STATE = 128
SSD_CHUNK = 128
SSD_GROUP_WIDTH = D_MODEL // SSD_GROUPS
LRU_BLOCKS = 16
LRU_BLOCK_DIM = 256
LRU_C = 8.0
XATTN_HEADS = 4
XATTN_HEAD_DIM = 128
PEER_HEADS = 8
PEER_N_KEYS = 128
PEER_TOPK = 16
DEEPNORM_ALPHA = 2.0 ** 0.25
LN_EPS = 1e-5
RMS_EPS = 1e-6
CONV_TAPS = 4
HIST_ROWS = 8

VMEM_LIMIT_BYTES = 58 * 1024 * 1024

COL_Z, COL_XS, COL_BC = 0, 4096, 8192
SSD_SLAB_COLS = 10240
COL_LRU_GATE, COL_LRU_X, COL_GATE_SSD, COL_GATE_LRU = 0, 4096, 8192, 12288
BC_WIDTH = 2 * SSD_GROUPS * SSD_STATE


def _params(sem, flags=None):
  return pltpu.CompilerParams(dimension_semantics=sem, vmem_limit_bytes=VMEM_LIMIT_BYTES,
                              flags=flags)


def _layer_norm(x, g, b):
  mu = jnp.mean(x, axis=-1, keepdims=True)
  xc = x - mu
  var = jnp.mean(xc * xc, axis=-1, keepdims=True)
  return xc * lax.rsqrt(var + LN_EPS) * g + b


def _dot(a, b):
  return jnp.dot(a, b, preferred_element_type=F32)


def _dot_nt(a, b):
  return lax.dot_general(a, b, (((1,), (1,)), ((), ())), preferred_element_type=F32)


def _ln_kernel(x_ref, g_ref, b_ref, o_ref, ob_ref):
  y = _layer_norm(x_ref[...], g_ref[...], b_ref[...])
  o_ref[...] = y
  ob_ref[...] = y.astype(BF16)


def _res_ln_kernel(h_ref, y_ref, g_ref, b_ref, o_ref, ob_ref):
  y = _layer_norm(DEEPNORM_ALPHA * h_ref[...] + y_ref[...], g_ref[...], b_ref[...])
  o_ref[...] = y
  ob_ref[...] = y.astype(BF16)


def _row_spec(bm, d):
  return pl.BlockSpec((bm, d), lambda i: (i, 0))


def _vec_spec(d):
  return pl.BlockSpec((1, d), lambda i: (0, 0))


def _ln_call(x, g, b, bm=256):
  t, d = x.shape
  return pl.pallas_call(
      _ln_kernel,
      grid=(t // bm,),
      in_specs=[_row_spec(bm, d), _vec_spec(d), _vec_spec(d)],
      out_specs=[_row_spec(bm, d), _row_spec(bm, d)],
      out_shape=[jax.ShapeDtypeStruct((t, d), F32), jax.ShapeDtypeStruct((t, d), BF16)],
      compiler_params=_params(("arbitrary",)),
      name="ln_in",
  )(x, g.reshape(1, d), b.reshape(1, d))


def _res_ln_call(h, y, g, b, name, bm=256):
  t, d = h.shape
  return pl.pallas_call(
      _res_ln_kernel,
      grid=(t // bm,),
      in_specs=[_row_spec(bm, d), _row_spec(bm, d), _vec_spec(d), _vec_spec(d)],
      out_specs=[_row_spec(bm, d), _row_spec(bm, d)],
      out_shape=[jax.ShapeDtypeStruct((t, d), F32), jax.ShapeDtypeStruct((t, d), BF16)],
      compiler_params=_params(("arbitrary",)),
      name=name,
  )(h, y, g.reshape(1, d), b.reshape(1, d))


def _mm_kernel(x_ref, w_ref, o_ref):
  o_ref[...] = _dot(x_ref[...], w_ref[...]).astype(o_ref.dtype)


def _matmul(x, w, out_dtype, name, bm=1024, bn=1024):
  m, k = x.shape
  n = w.shape[1]
  bm, bn = min(bm, m), min(bn, n)
  return pl.pallas_call(
      _mm_kernel,
      grid=(m // bm, n // bn),
      in_specs=[pl.BlockSpec((bm, k), lambda i, j: (i, 0)),
                pl.BlockSpec((k, bn), lambda i, j: (0, j))],
      out_specs=pl.BlockSpec((bm, bn), lambda i, j: (i, j)),
      out_shape=jax.ShapeDtypeStruct((m, n), out_dtype),
      compiler_params=_params(("arbitrary", "arbitrary")),
      name=name,
  )(x, w)


def _causal_conv(hist_ref, raw, w_ref, b_ref, cols=slice(None)):
  rows, c = raw.shape
  x3 = raw.astype(F32).reshape(rows // HIST_ROWS, HIST_ROWS, c)
  ext = jnp.concatenate([hist_ref[:, cols][None], x3], axis=0)
  hist_ref[:, cols] = x3[rows // HIST_ROWS - 1]
  sub = lax.broadcasted_iota(jnp.int32, x3.shape, 1)
  y = b_ref[:, cols] + w_ref[CONV_TAPS - 1:CONV_TAPS, cols] * x3
  for s in range(1, CONV_TAPS):
    rot = pltpu.roll(ext, s, axis=1)
    shifted = jnp.where(sub >= s, rot[1:], rot[:-1])
    y = y + w_ref[CONV_TAPS - 1 - s:CONV_TAPS - s, cols] * shifted
  return y


def _softplus(x):
  return jnp.maximum(x, 0.0) + jnp.log1p(jnp.exp(-jnp.abs(x)))


def _ssd_kernel(z_ref, x_ref, bc_ref, dt_ref, cwx_ref, cbx_ref, cwbc_ref, cbbc_ref,
                dtb_ref, alog_ref, dsk_ref, nw_ref, o_ref, xh_ref, bch_ref, st_ref):
  L = x_ref.shape[0]
  n = SSD_STATE
  gw = SSD_GROUP_WIDTH
  pair_w = 2 * SSD_HEAD_DIM
  hpg = SSD_HEADS // SSD_GROUPS

  @pl.when(pl.program_id(1) == 0)
  def _init():
    xh_ref[...] = jnp.zeros(xh_ref.shape, F32)
    bch_ref[...] = jnp.zeros(bch_ref.shape, F32)
    st_ref[...] = jnp.zeros(st_ref.shape, F32)

  dt = _softplus(dt_ref[...] + dtb_ref[...])
  acs = dt * (-jnp.exp(alog_ref[...]))
  row_t = lax.broadcasted_iota(jnp.int32, acs.shape, 0)
  shift = 1
  while shift < L:
    acs = acs + jnp.where(row_t >= shift, pltpu.roll(acs, shift, axis=0), 0.0)
    shift *= 2
  dt_t = dt.T
  acs_t = acs.T
  a_last = acs_t[:, L - 1:L]
  w_t = dt_t * jnp.exp(a_last - acs_t)
  dfs_all = jnp.exp(acs)
  cd_all = jnp.exp(a_last)

  row = lax.broadcasted_iota(jnp.int32, (L, L), 0)
  col = lax.broadcasted_iota(jnp.int32, (L, L), 1)
  causal = row >= col
  lo = lax.broadcasted_iota(jnp.int32, (L, pair_w), 1) < SSD_HEAD_DIM
  lo_n = lax.broadcasted_iota(jnp.int32, (n, pair_w), 1) < SSD_HEAD_DIM

  def conv_silu(hist_ref, raw_ref, w_ref, bias_ref, cols):
    y = _causal_conv(hist_ref, raw_ref[:, cols], w_ref, bias_ref, cols)
    y = y * jax.nn.sigmoid(y)
    return y.reshape(L, y.shape[2])

  for g in range(SSD_GROUPS):
    gcols = slice(g * gw, (g + 1) * gw)
    xs = conv_silu(xh_ref, x_ref, cwx_ref, cbx_ref, gcols)
    bm = conv_silu(bch_ref, bc_ref, cwbc_ref, cbbc_ref, slice(g * n, (g + 1) * n))
    cm = conv_silu(bch_ref, bc_ref, cwbc_ref, cbbc_ref,
                   slice((SSD_GROUPS + g) * n, (SSD_GROUPS + g + 1) * n))
    bm16 = bm.astype(BF16)
    cm16 = cm.astype(BF16)
    cb = _dot_nt(cm16, bm16)
    b_t = bm.T
    st = st_ref[g]
    y_off = _dot(cm16, st.astype(BF16))

    ys = []
    for q in range(gw // pair_w):
      sl = slice(q * pair_w, (q + 1) * pair_w)
      h0 = g * hpg + 2 * q
      xq = xs[:, sl]
      rhs = jnp.concatenate([jnp.where(lo, xq, 0.0), jnp.where(lo, 0.0, xq)],
                            axis=0).astype(BF16)
      m_parts, bw_parts = [], []
      for h in (h0, h0 + 1):
        seg = acs[:, h:h + 1] - acs_t[h:h + 1, :]
        lm = jnp.exp(jnp.where(causal, seg, -jnp.inf))
        m_parts.append((cb * lm * dt_t[h:h + 1, :]).astype(BF16))
        bw_parts.append((b_t * w_t[h:h + 1, :]).astype(BF16))
      lhs = jnp.concatenate([jnp.concatenate(m_parts, axis=1),
                             jnp.concatenate(bw_parts, axis=1)], axis=0)
      res = _dot(lhs, rhs)
      dfs = jnp.where(lo, dfs_all[:, h0:h0 + 1], dfs_all[:, h0 + 1:h0 + 2])
      ys.append(res[:L] + y_off[:, sl] * dfs)
      chunk_decay = jnp.where(lo_n, cd_all[h0:h0 + 1, :], cd_all[h0 + 1:h0 + 2, :])
      st_ref[g, :, sl] = st[:, sl] * chunk_decay + res[L:]

    y = jnp.concatenate(ys, axis=1) + dsk_ref[:, gcols] * xs
    zf = z_ref[:, gcols].astype(F32)
    y = y * (zf * jax.nn.sigmoid(zf))
    ms = jnp.mean(y * y, axis=-1, keepdims=True)
    o_ref[:, gcols] = (y * lax.rsqrt(ms + RMS_EPS) * nw_ref[:, gcols]).astype(BF16)


def _ssd_call(proj, dt_raw, conv_w, conv_b, dt_bias, a_log, d_skip, norm_w, batch, seq):
  t = proj.shape[0]
  L, n, d = SSD_CHUNK, SSD_STATE, D_MODEL
  nc = seq // L
  lanes = dt_raw.shape[1]
  pad = lanes - SSD_HEADS
  d_cols = jnp.repeat(d_skip, SSD_HEAD_DIM).reshape(1, d)

  def rows(b, c):
    return b * nc + c

  full = lambda r, c: pl.BlockSpec((r, c), lambda b, s: (0, 0))
  in_specs = [
      pl.BlockSpec((L, d), lambda b, c: (rows(b, c), COL_Z // d)),
      pl.BlockSpec((L, d), lambda b, c: (rows(b, c), COL_XS // d)),
      pl.BlockSpec((L, BC_WIDTH), lambda b, c: (rows(b, c), COL_BC // BC_WIDTH)),
      pl.BlockSpec((L, lanes), lambda b, c: (rows(b, c), 0)),
      full(CONV_TAPS, d), full(1, d), full(CONV_TAPS, BC_WIDTH), full(1, BC_WIDTH),
      full(1, lanes), full(1, lanes), full(1, d), full(1, d),
  ]
  return pl.pallas_call(
      _ssd_kernel,
      grid=(batch, nc),
      in_specs=in_specs,
      out_specs=pl.BlockSpec((L, d), lambda b, c: (rows(b, c), 0)),
      out_shape=jax.ShapeDtypeStruct((t, d), BF16),
      scratch_shapes=[
          pltpu.VMEM((HIST_ROWS, d), F32),
          pltpu.VMEM((HIST_ROWS, BC_WIDTH), F32),
          pltpu.VMEM((SSD_GROUPS, n, SSD_GROUP_WIDTH), F32),
      ],
      compiler_params=_params(("arbitrary", "arbitrary")),
      name="ssd_scan",
  )(proj, proj, proj, dt_raw, conv_w[:, :d], conv_b[:d].reshape(1, d), conv_w[:, d:],
    conv_b[d:].reshape(1, BC_WIDTH), jnp.pad(dt_bias, (0, pad)).reshape(1, lanes),
    jnp.pad(a_log, (0, pad)).reshape(1, lanes), d_cols, norm_w.reshape(1, d))


def _lru_kernel(x_ref, g_ref, cw_ref, cb_ref, wa_ref, ba_ref, wi_ref, bi_ref, lam_ref,
                o_ref, xh_ref, h_ref):
  ts, bd = x_ref.shape
  nt = ts // HIST_ROWS

  @pl.when(pl.program_id(2) == 0)
  def _init():
    xh_ref[...] = jnp.zeros(xh_ref.shape, F32)
    h_ref[...] = jnp.zeros(h_ref.shape, F32)

  xr3 = _causal_conv(xh_ref, x_ref[...], cw_ref, cb_ref)
  xr = xr3.reshape(ts, bd)
  xr16 = xr.astype(BF16)
  r = jax.nn.sigmoid(_dot(xr16, wa_ref[...]) + ba_ref[...])
  i = jax.nn.sigmoid(_dot(xr16, wi_ref[...]) + bi_ref[...])
  log_a = (-LRU_C) * r * _softplus(-lam_ref[...])
  a = jnp.exp(log_a)
  th = jnp.tanh(log_a)
  u = jnp.exp(0.5 * jnp.log(-2.0 * th / (1.0 - th))) * (i * xr)

  a = a.reshape(nt, HIST_ROWS, bd)
  u = u.reshape(nt, HIST_ROWS, bd)
  sub = lax.broadcasted_iota(jnp.int32, a.shape, 1)
  shift = 1
  while shift < HIST_ROWS:
    keep = sub >= shift
    a_prev = jnp.where(keep, pltpu.roll(a, shift, axis=1), 1.0)
    u_prev = jnp.where(keep, pltpu.roll(u, shift, axis=1), 0.0)
    u = a * u_prev + u
    a = a * a_prev
    shift *= 2
  carry = h_ref[...]
  hs = []
  for k in range(nt):
    hk = a[k] * carry + u[k]
    hs.append(hk)
    carry = jnp.broadcast_to(hk[HIST_ROWS - 1:HIST_ROWS, :], hk.shape)
  h_ref[...] = carry
  h = jnp.concatenate(hs, axis=0)
  o_ref[...] = (jax.nn.gelu(g_ref[...].astype(F32)) * h).astype(BF16)


def _lru_call(proj, conv_w, conv_b, w_a, b_a, w_i, b_i, lam, batch, seq, ts=512):
  t = proj.shape[0]
  bd = LRU_BLOCK_DIM
  nt = seq // ts

  def rows(b, k, s):
    return b * nt + s

  vec = lambda: pl.BlockSpec((1, bd), lambda b, k, s: (0, k))
  in_specs = [
      pl.BlockSpec((ts, bd), lambda b, k, s: (rows(b, k, s), COL_LRU_X // bd + k)),
      pl.BlockSpec((ts, bd), lambda b, k, s: (rows(b, k, s), COL_LRU_GATE // bd + k)),
      pl.BlockSpec((CONV_TAPS, bd), lambda b, k, s: (0, k)),
      vec(),
      pl.BlockSpec((None, bd, bd), lambda b, k, s: (k, 0, 0)),
      vec(),
      pl.BlockSpec((None, bd, bd), lambda b, k, s: (k, 0, 0)),
      vec(),
      vec(),
  ]
  return pl.pallas_call(
      _lru_kernel,
      grid=(batch, LRU_BLOCKS, nt),
      in_specs=in_specs,
      out_specs=pl.BlockSpec((ts, bd), lambda b, k, s: (rows(b, k, s), k)),
      out_shape=jax.ShapeDtypeStruct((t, D_MODEL), BF16),
      scratch_shapes=[pltpu.VMEM((HIST_ROWS, bd), F32), pltpu.VMEM((HIST_ROWS, bd), F32)],
      compiler_params=_params(("arbitrary", "arbitrary", "arbitrary")),
      name="rglru_scan",
  )(proj, proj, conv_w, conv_b.reshape(1, -1), w_a.astype(BF16), b_a.reshape(1, -1),
    w_i.astype(BF16), b_i.reshape(1, -1), lam.reshape(1, -1))


def _merge_kernel(ys_ref, yl_ref, ws_ref, wl_ref, ga_ref, gb_ref, o_ref):
  pa = _dot(ys_ref[...], ws_ref[...])
  pb = _dot(yl_ref[...], wl_ref[...])
  ga = jax.nn.sigmoid(ga_ref[...].astype(F32))
  gb = jax.nn.sigmoid(gb_ref[...].astype(F32))
  o_ref[...] = (ga * pa + gb * pb).astype(o_ref.dtype)


def _merge_call(y_ssd, y_lru, w_ssd, w_lru, proj, bm=512, bn=512):
  t, d = y_ssd.shape
  return pl.pallas_call(
      _merge_kernel,
      grid=(t // bm, d // bn),
      in_specs=[
          pl.BlockSpec((bm, d), lambda i, j: (i, 0)),
          pl.BlockSpec((bm, d), lambda i, j: (i, 0)),
          pl.BlockSpec((d, bn), lambda i, j: (0, j)),
          pl.BlockSpec((d, bn), lambda i, j: (0, j)),
          pl.BlockSpec((bm, bn), lambda i, j: (i, COL_GATE_SSD // bn + j)),
          pl.BlockSpec((bm, bn), lambda i, j: (i, COL_GATE_LRU // bn + j)),
      ],
      out_specs=pl.BlockSpec((bm, bn), lambda i, j: (i, j)),
      out_shape=jax.ShapeDtypeStruct((t, d), BF16),
      compiler_params=_params(("arbitrary", "arbitrary")),
      name="gated_merge",
  )(y_ssd, y_lru, w_ssd, w_lru, proj, proj)


def _xattn_kernel(h_ref, hb_ref, wq_ref, k_ref, v_ref, wo_ref, g_ref, b_ref, o_ref, ob_ref, obt_ref):
  hd = XATTN_HEAD_DIM
  q = _dot(hb_ref[...], wq_ref[...]).astype(BF16)
  k = k_ref[...]
  v = v_ref[...]
  outs = []
  for head in range(XATTN_HEADS):
    sl = slice(head * hd, (head + 1) * hd)
    s = _dot_nt(q[:, sl], k[:, sl]) * (hd ** -0.5)
    s = s - jnp.max(s, axis=-1, keepdims=True)
    p = jnp.exp(s)
    p = p / jnp.sum(p, axis=-1, keepdims=True)
    outs.append(_dot(p.astype(BF16), v[:, sl]))
  o = jnp.concatenate(outs, axis=1).astype(BF16)
  y = _dot(o, wo_ref[...])
  out = _layer_norm(DEEPNORM_ALPHA * h_ref[...] + y, g_ref[...], b_ref[...])
  o_ref[...] = out
  ob_ref[...] = out.astype(BF16)
  obt_ref[...] = out.T.astype(BF16)


def _xattn_call(h, hb, w_q, k, v, w_o, g, b, seq, tm=256):
  t, d = h.shape
  xw = w_q.shape[1]
  mem_len = k.shape[1]
  per_batch = seq // tm
  return pl.pallas_call(
      _xattn_kernel,
      grid=(t // tm,),
      in_specs=[
          _row_spec(tm, d), _row_spec(tm, d),
          pl.BlockSpec((d, xw), lambda i: (0, 0)),
          pl.BlockSpec((None, mem_len, xw), lambda i: (i // per_batch, 0, 0)),
          pl.BlockSpec((None, mem_len, xw), lambda i: (i // per_batch, 0, 0)),
          pl.BlockSpec((xw, d), lambda i: (0, 0)),
          _vec_spec(d), _vec_spec(d),
      ],
      out_specs=[_row_spec(tm, d), _row_spec(tm, d), pl.BlockSpec((d, tm), lambda i: (0, i))],
      out_shape=[jax.ShapeDtypeStruct((t, d), F32), jax.ShapeDtypeStruct((t, d), BF16),
                 jax.ShapeDtypeStruct((d, t), BF16)],
      compiler_params=_params(("arbitrary",)),
      name="mem_xattn_ln2",
  )(h, hb, w_q, k, v, w_o, g.reshape(1, d), b.reshape(1, d))


_PEER_CAND_PAIRS = tuple((a, b) for a in range(PEER_TOPK) for b in range(PEER_TOPK)
                         if (a + 1) * (b + 1) <= PEER_TOPK)
_PEER_CAND_ROWS = 56


def _peer_route_kernel(q_ref, k1_ref, k2_ref, s1_ref, s2_ref, st_ref, cand_ref):
  tm = q_ref.shape[0]
  kd = PEER_N_KEYS
  neg_inf = -jnp.inf

  def top_values(s):
    vals = []
    cur = s
    for _ in range(PEER_TOPK):
      m = jnp.max(cur, axis=0, keepdims=True)
      vals.append(m)
      cur = jnp.where(cur == m, neg_inf, cur)
    return vals

  cand_ref[...] = jnp.full(cand_ref.shape, neg_inf, F32)
  for head in range(PEER_HEADS):
    q1 = q_ref[:, head * 2 * kd:head * 2 * kd + kd]
    q2 = q_ref[:, head * 2 * kd + kd:(head + 1) * 2 * kd]
    s1 = _dot_nt(k1_ref[head], q1)
    s2 = _dot_nt(k2_ref[head], q2)
    s1_ref[head] = s1
    s2_ref[head] = s2
    v1 = top_values(s1)
    v2 = top_values(s2)
    for idx, (a, b) in enumerate(_PEER_CAND_PAIRS):
      cand_ref[idx:idx + 1, :] = v1[a] + v2[b]
    cand = cand_ref[...]
    cur = cand
    count = jnp.zeros((1, tm), F32)
    tau = jnp.full((1, tm), neg_inf, F32)
    for _ in range(PEER_TOPK):
      m = jnp.max(cur, axis=0, keepdims=True)
      eq = cur == m
      new_count = count + jnp.sum(jnp.where(eq, 1.0, 0.0), axis=0, keepdims=True)
      tau = jnp.where(count < PEER_TOPK, jnp.where(new_count >= PEER_TOPK, m, tau), tau)
      count = new_count
      cur = jnp.where(eq, neg_inf, cur)
    top = v1[0] + v2[0]
    z = jnp.sum(jnp.where(cand >= tau, jnp.exp(cand - top), 0.0), axis=0, keepdims=True)
    st_ref[0, head:head + 1, :] = tau
    st_ref[1, head:head + 1, :] = v1[0]
    st_ref[2, head:head + 1, :] = v2[0]
    st_ref[3, head:head + 1, :] = 1.0 / z


def _peer_route_call(q, keys_1, keys_2, tm=256):
  t = q.shape[0]
  h, kd = PEER_HEADS, PEER_N_KEYS
  score_spec = pl.BlockSpec((h, kd, tm), lambda i: (0, 0, i))
  return pl.pallas_call(
      _peer_route_kernel,
      grid=(t // tm,),
      in_specs=[pl.BlockSpec((tm, q.shape[1]), lambda i: (i, 0)),
                pl.BlockSpec((h, kd, kd), lambda i: (0, 0, 0)),
                pl.BlockSpec((h, kd, kd), lambda i: (0, 0, 0))],
      out_specs=[score_spec, score_spec, pl.BlockSpec((4, h, tm), lambda i: (0, 0, i))],
      out_shape=[jax.ShapeDtypeStruct((h, kd, t), F32), jax.ShapeDtypeStruct((h, kd, t), F32),
                 jax.ShapeDtypeStruct((4, h, t), F32)],
      scratch_shapes=[pltpu.VMEM((_PEER_CAND_ROWS, tm), F32)],
      compiler_params=_params(("arbitrary",)),
      name="peer_route",
  )(q, keys_1, keys_2)


def _peer_dense_kernel(xt_ref, u_ref, v_ref, s1_ref, s2_ref, st_ref, o_ref, e2_ref):
  j = pl.program_id(1)
  half = u_ref.shape[0] // 2
  kd = PEER_N_KEYS
  rows_per_half = half // kd

  def build_gates(half_index):
    parts = []
    for r in range(rows_per_half):
      i1 = half_index * rows_per_half + r
      g = None
      for head in range(PEER_HEADS):
        s1_row = s1_ref[head, pl.ds(i1, 1), :]
        e1_row = jnp.exp(s1_row - st_ref[1, head:head + 1, :]) * st_ref[3, head:head + 1, :]
        sel = (s2_ref[head] + s1_row) >= st_ref[0, head:head + 1, :]
        term = jnp.where(sel, e2_ref[head] * e1_row, 0.0)
        g = term if g is None else g + term
      parts.append(g)
    return jnp.concatenate(parts, axis=0)

  @pl.when(j == 0)
  def _init():
    o_ref[...] = jnp.zeros(o_ref.shape, F32)
    for head in range(PEER_HEADS):
      e2_ref[head] = jnp.exp(s2_ref[head] - st_ref[2, head:head + 1, :])

  xt = xt_ref[...]
  act_a = _dot(u_ref[0:half, :], xt)
  act_b = _dot(u_ref[half:2 * half, :], xt)
  p_a = (build_gates(2 * j) * jax.nn.gelu(act_a)).T.astype(BF16)
  o_ref[...] += _dot(p_a, v_ref[0:half, :])
  p_b = (build_gates(2 * j + 1) * jax.nn.gelu(act_b)).T.astype(BF16)
  o_ref[...] += _dot(p_b, v_ref[half:2 * half, :])


def _peer_dense_call(xt, u, v, s1, s2, stats, tm=512, te=512):
  d, t = xt.shape
  e = u.shape[0]
  h, kd = PEER_HEADS, PEER_N_KEYS
  score_spec = pl.BlockSpec((h, kd, tm), lambda i, j: (0, 0, i))
  return pl.pallas_call(
      _peer_dense_kernel,
      grid=(t // tm, e // te),
      in_specs=[pl.BlockSpec((d, tm), lambda i, j: (0, i)),
                pl.BlockSpec((te, d), lambda i, j: (j, 0)),
                pl.BlockSpec((te, d), lambda i, j: (j, 0)),
                score_spec, score_spec,
                pl.BlockSpec((4, h, tm), lambda i, j: (0, 0, i))],
      out_specs=pl.BlockSpec((tm, d), lambda i, j: (i, 0)),
      out_shape=jax.ShapeDtypeStruct((t, d), F32),
      scratch_shapes=[pltpu.VMEM((h, kd, tm), F32)],
      compiler_params=_params(("arbitrary", "arbitrary")),
      name="peer_dense",
  )(xt, u, v, s1, s2, stats)


def kernel(x, mem, ln_in_g, ln_in_b, w_in, ssd_conv_w, ssd_conv_b, ssd_dt_bias, ssd_a_log, ssd_d, ssd_norm_w, lru_conv_w, lru_conv_b, lru_w_a, lru_b_a, lru_w_i, lru_b_i, lru_lambda, w_proj_ssd, w_proj_lru, w_mix_out, ln1_g, ln1_b, xa_w_q, xa_w_k, xa_w_v, xa_w_o, ln2_g, ln2_b, peer_w_q, peer_keys_1, peer_keys_2, peer_u, peer_v, ln3_g, ln3_b):
  batch, seq, d = x.shape
  t = batch * seq
  assert w_in.shape[0] == 1, "single-layer trunk"

  w = w_in[0]
  c_dt = SSD_SLAB_COLS
  c_lru = c_dt + SSD_HEADS
  w_ssd = w[:, :c_dt].astype(BF16)
  w_dt = jnp.pad(w[:, c_dt:c_lru], ((0, 0), (0, 128 - SSD_HEADS))).astype(BF16)
  w_lru = w[:, c_lru:].astype(BF16)

  h0, h0b = _ln_call(x.reshape(t, d), ln_in_g, ln_in_b)
  proj_ssd = _matmul(h0b, w_ssd, BF16, "in_proj_ssd")
  proj_lru = _matmul(h0b, w_lru, BF16, "in_proj_lru")
  dt_raw = _matmul(h0b, w_dt, F32, "dt_proj")

  y_ssd = _ssd_call(proj_ssd, dt_raw, ssd_conv_w[0], ssd_conv_b[0], ssd_dt_bias[0], ssd_a_log[0],
                    ssd_d[0], ssd_norm_w[0], batch, seq)
  y_lru = _lru_call(proj_lru, lru_conv_w[0], lru_conv_b[0], lru_w_a[0], lru_b_a[0], lru_w_i[0],
                    lru_b_i[0], lru_lambda[0], batch, seq)
  merged = _merge_call(y_ssd, y_lru, w_proj_ssd[0].astype(BF16), w_proj_lru[0].astype(BF16),
                       proj_lru)
  mixed = _matmul(merged, w_mix_out[0].astype(BF16), F32, "mix_out")
  h1, h1b = _res_ln_call(h0, mixed, ln1_g[0], ln1_b[0], "res_ln1")

  mem_len = mem.shape[1]
  w_kv = jnp.concatenate([xa_w_k[0], xa_w_v[0]], axis=1).astype(BF16)
  kv = _matmul(mem.reshape(batch * mem_len, d).astype(BF16), w_kv, BF16, "mem_kv")
  xw = xa_w_k.shape[-1]
  k = kv[:, :xw].reshape(batch, mem_len, xw)
  v = kv[:, xw:].reshape(batch, mem_len, xw)
  h2, h2b, h2bt = _xattn_call(h1, h1b, xa_w_q[0].astype(BF16), k, v, xa_w_o[0].astype(BF16),
                              ln2_g[0], ln2_b[0], seq)

  q = _matmul(h2b, peer_w_q[0].astype(BF16), BF16, "peer_query")
  s1, s2, stats = _peer_route_call(q, peer_keys_1[0].astype(BF16), peer_keys_2[0].astype(BF16))
  ffn = _peer_dense_call(h2bt, peer_u[0].astype(BF16), peer_v[0].astype(BF16), s1, s2, stats)
  h3, _ = _res_ln_call(h2, ffn, ln3_g[0], ln3_b[0], "res_ln3")
  return h3.reshape(batch, seq, d)
```

```python
import functools

import jax
import jax.numpy as jnp
from jax import lax
from jax.experimental import pallas as pl
from jax.experimental.pallas import tpu as pltpu

F32 = jnp.float32
BF16 = jnp.bfloat16
FP8 = jnp.float8_e4m3fn
FP8_TARGET = 384.0
FP8_AMAX_FLOOR = 1e-30

D_MODEL = 4096
SSD_HEADS = 64
SSD_HEAD_DIM = 64
SSD_GROUPS = 8
SSD_STATE = 128
SSD_CHUNK = 128
SSD_GROUP_WIDTH = D_MODEL // SSD_GROUPS
LRU_BLOCKS = 16
LRU_BLOCK_DIM = 256
LRU_C = 8.0
XATTN_HEADS = 4
XATTN_HEAD_DIM = 128
PEER_HEADS = 8
PEER_N_KEYS = 128
PEER_TOPK = 16
DEEPNORM_ALPHA = 2.0 ** 0.25
LN_EPS = 1e-5
RMS_EPS = 1e-6
CONV_TAPS = 4
HIST_ROWS = 8

VMEM_LIMIT_BYTES = 58 * 1024 * 1024

COL_Z, COL_XS, COL_BC = 0, 4096, 8192
SSD_SLAB_COLS = 10240
COL_LRU_GATE, COL_LRU_X, COL_GATE_SSD, COL_GATE_LRU = 0, 4096, 8192, 12288
BC_WIDTH = 2 * SSD_GROUPS * SSD_STATE


def _params(sem, flags=None):
  return pltpu.CompilerParams(dimension_semantics=sem, vmem_limit_bytes=VMEM_LIMIT_BYTES,
                              flags=flags)


def _layer_norm(x, g, b):
  mu = jnp.mean(x, axis=-1, keepdims=True)
  xc = x - mu
  var = jnp.mean(xc * xc, axis=-1, keepdims=True)
  return xc * lax.rsqrt(var + LN_EPS) * g + b


def _dot(a, b):
  return jnp.dot(a, b, preferred_element_type=F32)


def _dot_nt(a, b):
  return lax.dot_general(a, b, (((1,), (1,)), ((), ())), preferred_element_type=F32)


SCALE_LANES = 128


def _quantize_rows(y, q_ref, s_ref):
  amax = jnp.maximum(jnp.max(jnp.abs(y), axis=-1, keepdims=True), FP8_AMAX_FLOOR)
  q_ref[...] = (y * (FP8_TARGET / amax)).astype(FP8)
  s_ref[...] = jnp.broadcast_to(amax * (1.0 / FP8_TARGET), s_ref.shape)


def _ln_kernel(x_ref, g_ref, b_ref, o_ref, ob_ref, oq_ref, os_ref):
  y = _layer_norm(x_ref[...], g_ref[...], b_ref[...])
  o_ref[...] = y
  ob_ref[...] = y.astype(BF16)
  _quantize_rows(y, oq_ref, os_ref)


def _res_ln_kernel(h_ref, y_ref, g_ref, b_ref, o_ref, ob_ref):
  y = _layer_norm(DEEPNORM_ALPHA * h_ref[...] + y_ref[...], g_ref[...], b_ref[...])
  o_ref[...] = y
  ob_ref[...] = y.astype(BF16)


def _row_spec(bm, d):
  return pl.BlockSpec((bm, d), lambda i: (i, 0))


def _vec_spec(d):
  return pl.BlockSpec((1, d), lambda i: (0, 0))


def _ln_call(x, g, b, bm=256):
  t, d = x.shape
  return pl.pallas_call(
      _ln_kernel,
      grid=(t // bm,),
      in_specs=[_row_spec(bm, d), _vec_spec(d), _vec_spec(d)],
      out_specs=[_row_spec(bm, d), _row_spec(bm, d), _row_spec(bm, d),
                 _row_spec(bm, SCALE_LANES)],
      out_shape=[jax.ShapeDtypeStruct((t, d), F32), jax.ShapeDtypeStruct((t, d), BF16),
                 jax.ShapeDtypeStruct((t, d), FP8), jax.ShapeDtypeStruct((t, SCALE_LANES), F32)],
      compiler_params=_params(("arbitrary",)),
      name="ln_in",
  )(x, g.reshape(1, d), b.reshape(1, d))


def _res_ln_call(h, y, g, b, name, bm=256):
  t, d = h.shape
  return pl.pallas_call(
      _res_ln_kernel,
      grid=(t // bm,),
      in_specs=[_row_spec(bm, d), _row_spec(bm, d), _vec_spec(d), _vec_spec(d)],
      out_specs=[_row_spec(bm, d), _row_spec(bm, d)],
      out_shape=[jax.ShapeDtypeStruct((t, d), F32), jax.ShapeDtypeStruct((t, d), BF16)],
      compiler_params=_params(("arbitrary",)),
      name=name,
  )(h, y, g.reshape(1, d), b.reshape(1, d))


def _mm_kernel(x_ref, w_ref, o_ref):
  o_ref[...] = _dot(x_ref[...], w_ref[...]).astype(o_ref.dtype)


def _mm8_kernel(x_ref, xs_ref, w_ref, ws_ref, o_ref):
  acc = _dot(x_ref[...], w_ref[...])
  xs = pltpu.repeat(xs_ref[...], acc.shape[1] // xs_ref.shape[1], axis=1)
  o_ref[...] = (acc * xs * ws_ref[...]).astype(o_ref.dtype)


def _matmul_fp8(x8, x_scale, w, out_dtype, name, bm=1024, bn=1024):
  m, k = x8.shape
  n = w.shape[1]
  bm, bn = min(bm, m), min(bn, n)
  w_amax = jnp.maximum(jnp.max(jnp.abs(w), axis=0, keepdims=True), FP8_AMAX_FLOOR)
  w8 = (w * (FP8_TARGET / w_amax)).astype(FP8)
  w_scale = w_amax * (1.0 / FP8_TARGET)
  return pl.pallas_call(
      _mm8_kernel,
      grid=(m // bm, n // bn),
      in_specs=[pl.BlockSpec((bm, k), lambda i, j: (i, 0)),
                pl.BlockSpec((bm, x_scale.shape[1]), lambda i, j: (i, 0)),
                pl.BlockSpec((k, bn), lambda i, j: (0, j)),
                pl.BlockSpec((1, bn), lambda i, j: (0, j))],
      out_specs=pl.BlockSpec((bm, bn), lambda i, j: (i, j)),
      out_shape=jax.ShapeDtypeStruct((m, n), out_dtype),
      compiler_params=_params(("arbitrary", "arbitrary")),
      name=name,
  )(x8, x_scale, w8, w_scale)


def _matmul(x, w, out_dtype, name, bm=1024, bn=1024):
  m, k = x.shape
  n = w.shape[1]
  bm, bn = min(bm, m), min(bn, n)
  return pl.pallas_call(
      _mm_kernel,
      grid=(m // bm, n // bn),
      in_specs=[pl.BlockSpec((bm, k), lambda i, j: (i, 0)),
                pl.BlockSpec((k, bn), lambda i, j: (0, j))],
      out_specs=pl.BlockSpec((bm, bn), lambda i, j: (i, j)),
      out_shape=jax.ShapeDtypeStruct((m, n), out_dtype),
      compiler_params=_params(("arbitrary", "arbitrary")),
      name=name,
  )(x, w)


def _causal_conv(hist_ref, raw, w_ref, b_ref, cols=slice(None)):
  rows, c = raw.shape
  x3 = raw.astype(F32).reshape(rows // HIST_ROWS, HIST_ROWS, c)
  ext = jnp.concatenate([hist_ref[:, cols][None], x3], axis=0)
  hist_ref[:, cols] = x3[rows // HIST_ROWS - 1]
  sub = lax.broadcasted_iota(jnp.int32, x3.shape, 1)
  y = b_ref[:, cols] + w_ref[CONV_TAPS - 1:CONV_TAPS, cols] * x3
  for s in range(1, CONV_TAPS):
    rot = pltpu.roll(ext, s, axis=1)
    shifted = jnp.where(sub >= s, rot[1:], rot[:-1])
    y = y + w_ref[CONV_TAPS - 1 - s:CONV_TAPS - s, cols] * shifted
  return y


def _softplus(x):
  return jnp.maximum(x, 0.0) + jnp.log1p(jnp.exp(-jnp.abs(x)))


def _ssd_kernel(z_ref, x_ref, bc_ref, dt_ref, cwx_ref, cbx_ref, cwbc_ref, cbbc_ref,
                dtb_ref, alog_ref, dsk_ref, nw_ref, o_ref, xh_ref, bch_ref, st_ref):
  L = x_ref.shape[0]
  n = SSD_STATE
  gw = SSD_GROUP_WIDTH
  pair_w = 2 * SSD_HEAD_DIM
  hpg = SSD_HEADS // SSD_GROUPS

  @pl.when(pl.program_id(1) == 0)
  def _init():
    xh_ref[...] = jnp.zeros(xh_ref.shape, F32)
    bch_ref[...] = jnp.zeros(bch_ref.shape, F32)
    st_ref[...] = jnp.zeros(st_ref.shape, F32)

  dt = _softplus(dt_ref[...] + dtb_ref[...])
  acs = dt * (-jnp.exp(alog_ref[...]))
  row_t = lax.broadcasted_iota(jnp.int32, acs.shape, 0)
  shift = 1
  while shift < L:
    acs = acs + jnp.where(row_t >= shift, pltpu.roll(acs, shift, axis=0), 0.0)
    shift *= 2
  dt_t = dt.T
  acs_t = acs.T
  a_last = acs_t[:, L - 1:L]
  w_t = dt_t * jnp.exp(a_last - acs_t)
  dfs_all = jnp.exp(acs)
  cd_all = jnp.exp(a_last)

  row = lax.broadcasted_iota(jnp.int32, (L, L), 0)
  col = lax.broadcasted_iota(jnp.int32, (L, L), 1)
  causal = row >= col
  lo = lax.broadcasted_iota(jnp.int32, (L, pair_w), 1) < SSD_HEAD_DIM
  lo_n = lax.broadcasted_iota(jnp.int32, (n, pair_w), 1) < SSD_HEAD_DIM

  def conv_silu(hist_ref, raw_ref, w_ref, bias_ref, cols):
    y = _causal_conv(hist_ref, raw_ref[:, cols], w_ref, bias_ref, cols)
    y = y * jax.nn.sigmoid(y)
    return y.reshape(L, y.shape[2])

  for g in range(SSD_GROUPS):
    gcols = slice(g * gw, (g + 1) * gw)
    xs = conv_silu(xh_ref, x_ref, cwx_ref, cbx_ref, gcols)
    bm = conv_silu(bch_ref, bc_ref, cwbc_ref, cbbc_ref, slice(g * n, (g + 1) * n))
    cm = conv_silu(bch_ref, bc_ref, cwbc_ref, cbbc_ref,
                   slice((SSD_GROUPS + g) * n, (SSD_GROUPS + g + 1) * n))
    bm16 = bm.astype(BF16)
    cm16 = cm.astype(BF16)
    cb = _dot_nt(cm16, bm16)
    b_t = bm.T
    st = st_ref[g]
    y_off = _dot(cm16, st.astype(BF16))

    ys = []
    for q in range(gw // pair_w):
      sl = slice(q * pair_w, (q + 1) * pair_w)
      h0 = g * hpg + 2 * q
      xq = xs[:, sl]
      rhs = jnp.concatenate([jnp.where(lo, xq, 0.0), jnp.where(lo, 0.0, xq)],
                            axis=0).astype(BF16)
      m_parts, bw_parts = [], []
      for h in (h0, h0 + 1):
        seg = acs[:, h:h + 1] - acs_t[h:h + 1, :]
        lm = jnp.exp(jnp.where(causal, seg, -jnp.inf))
        m_parts.append((cb * lm * dt_t[h:h + 1, :]).astype(BF16))
        bw_parts.append((b_t * w_t[h:h + 1, :]).astype(BF16))
      lhs = jnp.concatenate([jnp.concatenate(m_parts, axis=1),
                             jnp.concatenate(bw_parts, axis=1)], axis=0)
      res = _dot(lhs, rhs)
      dfs = jnp.where(lo, dfs_all[:, h0:h0 + 1], dfs_all[:, h0 + 1:h0 + 2])
      ys.append(res[:L] + y_off[:, sl] * dfs)
      chunk_decay = jnp.where(lo_n, cd_all[h0:h0 + 1, :], cd_all[h0 + 1:h0 + 2, :])
      st_ref[g, :, sl] = st[:, sl] * chunk_decay + res[L:]

    y = jnp.concatenate(ys, axis=1) + dsk_ref[:, gcols] * xs
    zf = z_ref[:, gcols].astype(F32)
    y = y * (zf * jax.nn.sigmoid(zf))
    ms = jnp.mean(y * y, axis=-1, keepdims=True)
    o_ref[:, gcols] = (y * lax.rsqrt(ms + RMS_EPS) * nw_ref[:, gcols]).astype(BF16)


def _ssd_call(proj, dt_raw, conv_w, conv_b, dt_bias, a_log, d_skip, norm_w, batch, seq):
  t = proj.shape[0]
  L, n, d = SSD_CHUNK, SSD_STATE, D_MODEL
  nc = seq // L
  lanes = dt_raw.shape[1]
  pad = lanes - SSD_HEADS
  d_cols = jnp.repeat(d_skip, SSD_HEAD_DIM).reshape(1, d)

  def rows(b, c):
    return b * nc + c

  full = lambda r, c: pl.BlockSpec((r, c), lambda b, s: (0, 0))
  in_specs = [
      pl.BlockSpec((L, d), lambda b, c: (rows(b, c), COL_Z // d)),
      pl.BlockSpec((L, d), lambda b, c: (rows(b, c), COL_XS // d)),
      pl.BlockSpec((L, BC_WIDTH), lambda b, c: (rows(b, c), COL_BC // BC_WIDTH)),
      pl.BlockSpec((L, lanes), lambda b, c: (rows(b, c), 0)),
      full(CONV_TAPS, d), full(1, d), full(CONV_TAPS, BC_WIDTH), full(1, BC_WIDTH),
      full(1, lanes), full(1, lanes), full(1, d), full(1, d),
  ]
  return pl.pallas_call(
      _ssd_kernel,
      grid=(batch, nc),
      in_specs=in_specs,
      out_specs=pl.BlockSpec((L, d), lambda b, c: (rows(b, c), 0)),
      out_shape=jax.ShapeDtypeStruct((t, d), BF16),
      scratch_shapes=[
          pltpu.VMEM((HIST_ROWS, d), F32),
          pltpu.VMEM((HIST_ROWS, BC_WIDTH), F32),
          pltpu.VMEM((SSD_GROUPS, n, SSD_GROUP_WIDTH), F32),
      ],
      compiler_params=_params(("arbitrary", "arbitrary")),
      name="ssd_scan",
  )(proj, proj, proj, dt_raw, conv_w[:, :d], conv_b[:d].reshape(1, d), conv_w[:, d:],
    conv_b[d:].reshape(1, BC_WIDTH), jnp.pad(dt_bias, (0, pad)).reshape(1, lanes),
    jnp.pad(a_log, (0, pad)).reshape(1, lanes), d_cols, norm_w.reshape(1, d))


def _lru_kernel(x_ref, g_ref, cw_ref, cb_ref, wa_ref, ba_ref, wi_ref, bi_ref, lam_ref,
                o_ref, xh_ref, h_ref):
  ts, bd = x_ref.shape
  nt = ts // HIST_ROWS

  @pl.when(pl.program_id(2) == 0)
  def _init():
    xh_ref[...] = jnp.zeros(xh_ref.shape, F32)
    h_ref[...] = jnp.zeros(h_ref.shape, F32)

  xr3 = _causal_conv(xh_ref, x_ref[...], cw_ref, cb_ref)
  xr = xr3.reshape(ts, bd)
  xr16 = xr.astype(BF16)
  r = jax.nn.sigmoid(_dot(xr16, wa_ref[...]) + ba_ref[...])
  i = jax.nn.sigmoid(_dot(xr16, wi_ref[...]) + bi_ref[...])
  log_a = (-LRU_C) * r * _softplus(-lam_ref[...])
  a = jnp.exp(log_a)
  th = jnp.tanh(log_a)
  u = jnp.exp(0.5 * jnp.log(-2.0 * th / (1.0 - th))) * (i * xr)

  a = a.reshape(nt, HIST_ROWS, bd)
  u = u.reshape(nt, HIST_ROWS, bd)
  sub = lax.broadcasted_iota(jnp.int32, a.shape, 1)
  shift = 1
  while shift < HIST_ROWS:
    keep = sub >= shift
    a_prev = jnp.where(keep, pltpu.roll(a, shift, axis=1), 1.0)
    u_prev = jnp.where(keep, pltpu.roll(u, shift, axis=1), 0.0)
    u = a * u_prev + u
    a = a * a_prev
    shift *= 2
  carry = h_ref[...]
  hs = []
  for k in range(nt):
    hk = a[k] * carry + u[k]
    hs.append(hk)
    carry = jnp.broadcast_to(hk[HIST_ROWS - 1:HIST_ROWS, :], hk.shape)
  h_ref[...] = carry
  h = jnp.concatenate(hs, axis=0)
  o_ref[...] = (jax.nn.gelu(g_ref[...].astype(F32)) * h).astype(BF16)


def _lru_call(proj, conv_w, conv_b, w_a, b_a, w_i, b_i, lam, batch, seq, ts=512):
  t = proj.shape[0]
  bd = LRU_BLOCK_DIM
  nt = seq // ts

  def rows(b, k, s):
    return b * nt + s

  vec = lambda: pl.BlockSpec((1, bd), lambda b, k, s: (0, k))
  in_specs = [
      pl.BlockSpec((ts, bd), lambda b, k, s: (rows(b, k, s), COL_LRU_X // bd + k)),
      pl.BlockSpec((ts, bd), lambda b, k, s: (rows(b, k, s), COL_LRU_GATE // bd + k)),
      pl.BlockSpec((CONV_TAPS, bd), lambda b, k, s: (0, k)),
      vec(),
      pl.BlockSpec((None, bd, bd), lambda b, k, s: (k, 0, 0)),
      vec(),
      pl.BlockSpec((None, bd, bd), lambda b, k, s: (k, 0, 0)),
      vec(),
      vec(),
  ]
  return pl.pallas_call(
      _lru_kernel,
      grid=(batch, LRU_BLOCKS, nt),
      in_specs=in_specs,
      out_specs=pl.BlockSpec((ts, bd), lambda b, k, s: (rows(b, k, s), k)),
      out_shape=jax.ShapeDtypeStruct((t, D_MODEL), BF16),
      scratch_shapes=[pltpu.VMEM((HIST_ROWS, bd), F32), pltpu.VMEM((HIST_ROWS, bd), F32)],
      compiler_params=_params(("arbitrary", "arbitrary", "arbitrary")),
      name="rglru_scan",
  )(proj, proj, conv_w, conv_b.reshape(1, -1), w_a.astype(BF16), b_a.reshape(1, -1),
    w_i.astype(BF16), b_i.reshape(1, -1), lam.reshape(1, -1))


def _merge_kernel(ys_ref, yl_ref, ws_ref, wl_ref, ga_ref, gb_ref, o_ref):
  pa = _dot(ys_ref[...], ws_ref[...])
  pb = _dot(yl_ref[...], wl_ref[...])
  ga = jax.nn.sigmoid(ga_ref[...].astype(F32))
  gb = jax.nn.sigmoid(gb_ref[...].astype(F32))
  o_ref[...] = (ga * pa + gb * pb).astype(o_ref.dtype)


def _merge_call(y_ssd, y_lru, w_ssd, w_lru, proj, bm=512, bn=512):
  t, d = y_ssd.shape
  return pl.pallas_call(
      _merge_kernel,
      grid=(t // bm, d // bn),
      in_specs=[
          pl.BlockSpec((bm, d), lambda i, j: (i, 0)),
          pl.BlockSpec((bm, d), lambda i, j: (i, 0)),
          pl.BlockSpec((d, bn), lambda i, j: (0, j)),
          pl.BlockSpec((d, bn), lambda i, j: (0, j)),
          pl.BlockSpec((bm, bn), lambda i, j: (i, COL_GATE_SSD // bn + j)),
          pl.BlockSpec((bm, bn), lambda i, j: (i, COL_GATE_LRU // bn + j)),
      ],
      out_specs=pl.BlockSpec((bm, bn), lambda i, j: (i, j)),
      out_shape=jax.ShapeDtypeStruct((t, d), BF16),
      compiler_params=_params(("arbitrary", "arbitrary")),
      name="gated_merge",
  )(y_ssd, y_lru, w_ssd, w_lru, proj, proj)


def _xattn_kernel(h_ref, hb_ref, wq_ref, k_ref, v_ref, wo_ref, g_ref, b_ref, o_ref, ob_ref,
                  obt_ref, osc_ref):
  hd = XATTN_HEAD_DIM
  q = _dot(hb_ref[...], wq_ref[...]).astype(BF16)
  k = k_ref[...]
  v = v_ref[...]
  outs = []
  for head in range(XATTN_HEADS):
    sl = slice(head * hd, (head + 1) * hd)
    s = _dot_nt(q[:, sl], k[:, sl]) * (hd ** -0.5)
    s = s - jnp.max(s, axis=-1, keepdims=True)
    p = jnp.exp(s)
    p = p / jnp.sum(p, axis=-1, keepdims=True)
    outs.append(_dot(p.astype(BF16), v[:, sl]))
  o = jnp.concatenate(outs, axis=1).astype(BF16)
  y = _dot(o, wo_ref[...])
  out = _layer_norm(DEEPNORM_ALPHA * h_ref[...] + y, g_ref[...], b_ref[...])
  o_ref[...] = out
  ob_ref[...] = out.astype(BF16)
  out_t = out.T
  amax = jnp.maximum(jnp.max(jnp.abs(out_t), axis=0, keepdims=True), FP8_AMAX_FLOOR)
  obt_ref[...] = (out_t * (FP8_TARGET / amax)).astype(FP8)
  osc_ref[...] = amax * (1.0 / FP8_TARGET)


def _xattn_call(h, hb, w_q, k, v, w_o, g, b, seq, tm=256):
  t, d = h.shape
  xw = w_q.shape[1]
  mem_len = k.shape[1]
  per_batch = seq // tm
  return pl.pallas_call(
      _xattn_kernel,
      grid=(t // tm,),
      in_specs=[
          _row_spec(tm, d), _row_spec(tm, d),
          pl.BlockSpec((d, xw), lambda i: (0, 0)),
          pl.BlockSpec((None, mem_len, xw), lambda i: (i // per_batch, 0, 0)),
          pl.BlockSpec((None, mem_len, xw), lambda i: (i // per_batch, 0, 0)),
          pl.BlockSpec((xw, d), lambda i: (0, 0)),
          _vec_spec(d), _vec_spec(d),
      ],
      out_specs=[_row_spec(tm, d), _row_spec(tm, d), pl.BlockSpec((d, tm), lambda i: (0, i)),
                 pl.BlockSpec((1, tm), lambda i: (0, i))],
      out_shape=[jax.ShapeDtypeStruct((t, d), F32), jax.ShapeDtypeStruct((t, d), BF16),
                 jax.ShapeDtypeStruct((d, t), FP8), jax.ShapeDtypeStruct((1, t), F32)],
      compiler_params=_params(("arbitrary",)),
      name="mem_xattn_ln2",
  )(h, hb, w_q, k, v, w_o, g.reshape(1, d), b.reshape(1, d))


_PEER_CAND_PAIRS = tuple((a, b) for a in range(PEER_TOPK) for b in range(PEER_TOPK)
                         if (a + 1) * (b + 1) <= PEER_TOPK)
_PEER_CAND_ROWS = 56


def _peer_route_kernel(q_ref, k1_ref, k2_ref, s1_ref, s2_ref, st_ref, cand_ref):
  tm = q_ref.shape[0]
  kd = PEER_N_KEYS
  neg_inf = -jnp.inf

  def top_values(s):
    vals = []
    cur = s
    for _ in range(PEER_TOPK):
      m = jnp.max(cur, axis=0, keepdims=True)
      vals.append(m)
      cur = jnp.where(cur == m, neg_inf, cur)
    return vals

  cand_ref[...] = jnp.full(cand_ref.shape, neg_inf, F32)
  for head in range(PEER_HEADS):
    q1 = q_ref[:, head * 2 * kd:head * 2 * kd + kd]
    q2 = q_ref[:, head * 2 * kd + kd:(head + 1) * 2 * kd]
    s1 = _dot_nt(k1_ref[head], q1)
    s2 = _dot_nt(k2_ref[head], q2)
    s1_ref[head] = s1
    s2_ref[head] = s2
    v1 = top_values(s1)
    v2 = top_values(s2)
    for idx, (a, b) in enumerate(_PEER_CAND_PAIRS):
      cand_ref[idx:idx + 1, :] = v1[a] + v2[b]
    cand = cand_ref[...]
    cur = cand
    count = jnp.zeros((1, tm), F32)
    tau = jnp.full((1, tm), neg_inf, F32)
    for _ in range(PEER_TOPK):
      m = jnp.max(cur, axis=0, keepdims=True)
      eq = cur == m
      new_count = count + jnp.sum(jnp.where(eq, 1.0, 0.0), axis=0, keepdims=True)
      tau = jnp.where(count < PEER_TOPK, jnp.where(new_count >= PEER_TOPK, m, tau), tau)
      count = new_count
      cur = jnp.where(eq, neg_inf, cur)
    top = v1[0] + v2[0]
    z = jnp.sum(jnp.where(cand >= tau, jnp.exp(cand - top), 0.0), axis=0, keepdims=True)
    st_ref[0, head:head + 1, :] = tau
    st_ref[1, head:head + 1, :] = v1[0]
    st_ref[2, head:head + 1, :] = v2[0]
    st_ref[3, head:head + 1, :] = 1.0 / z


def _peer_route_call(q, keys_1, keys_2, tm=256):
  t = q.shape[0]
  h, kd = PEER_HEADS, PEER_N_KEYS
  score_spec = pl.BlockSpec((h, kd, tm), lambda i: (0, 0, i))
  return pl.pallas_call(
      _peer_route_kernel,
      grid=(t // tm,),
      in_specs=[pl.BlockSpec((tm, q.shape[1]), lambda i: (i, 0)),
                pl.BlockSpec((h, kd, kd), lambda i: (0, 0, 0)),
                pl.BlockSpec((h, kd, kd), lambda i: (0, 0, 0))],
      out_specs=[score_spec, score_spec, pl.BlockSpec((4, h, tm), lambda i: (0, 0, i))],
      out_shape=[jax.ShapeDtypeStruct((h, kd, t), F32), jax.ShapeDtypeStruct((h, kd, t), F32),
                 jax.ShapeDtypeStruct((4, h, t), F32)],
      scratch_shapes=[pltpu.VMEM((_PEER_CAND_ROWS, tm), F32)],
      compiler_params=_params(("arbitrary",)),
      name="peer_route",
  )(q, keys_1, keys_2)


def _peer_dense_kernel(xt_ref, xsc_ref, u_ref, usc_ref, v_ref, s1_ref, s2_ref, st_ref, o_ref,
                       e2_ref):
  j = pl.program_id(1)
  tm = xt_ref.shape[1]
  half = u_ref.shape[0] // 2
  kd = PEER_N_KEYS
  rows_per_half = half // kd

  def gate_term(i1, head):
    s1_row = s1_ref[head, pl.ds(i1, 1), :]
    e1_row = jnp.exp(s1_row - st_ref[1, head:head + 1, :]) * st_ref[3, head:head + 1, :]
    sel = (s2_ref[head] + s1_row) >= st_ref[0, head:head + 1, :]
    return jnp.where(sel, e2_ref[head] * e1_row, 0.0)

  @pl.when(j == 0)
  def _init():
    o_ref[...] = jnp.zeros(o_ref.shape, F32)
    for head in range(PEER_HEADS):
      e2_ref[head] = jnp.exp(s2_ref[head] - st_ref[2, head:head + 1, :])

  def half_tile(half_index):
    rows = slice(half_index * half, (half_index + 1) * half)
    act = _dot(u_ref[rows, :], xt_ref[...])
    act = act * pltpu.repeat(usc_ref[rows, :], tm // usc_ref.shape[1], axis=1) * xsc_ref[...]
    gates = []
    for r in range(rows_per_half):
      i1 = (2 * j + half_index) * rows_per_half + r
      g = gate_term(i1, 0)
      for head in range(1, PEER_HEADS):
        g = g + gate_term(i1, head)
      gates.append(g)
    p = (jnp.concatenate(gates, axis=0) * jax.nn.gelu(act)).T.astype(BF16)
    o_ref[...] += _dot(p, v_ref[rows, :])

  half_tile(0)
  half_tile(1)


def _peer_dense_call(xt, xsc, u, usc, v, s1, s2, stats, tm=512, te=512):
  d, t = xt.shape
  e = u.shape[0]
  h, kd = PEER_HEADS, PEER_N_KEYS
  score_spec = pl.BlockSpec((h, kd, tm), lambda i, j: (0, 0, i))
  return pl.pallas_call(
      _peer_dense_kernel,
      grid=(t // tm, e // te),
      in_specs=[pl.BlockSpec((d, tm), lambda i, j: (0, i)),
                pl.BlockSpec((1, tm), lambda i, j: (0, i)),
                pl.BlockSpec((te, d), lambda i, j: (j, 0)),
                pl.BlockSpec((te, usc.shape[1]), lambda i, j: (j, 0)),
                pl.BlockSpec((te, d), lambda i, j: (j, 0)),
                score_spec, score_spec,
                pl.BlockSpec((4, h, tm), lambda i, j: (0, 0, i))],
      out_specs=pl.BlockSpec((tm, d), lambda i, j: (i, 0)),
      out_shape=jax.ShapeDtypeStruct((t, d), F32),
      scratch_shapes=[pltpu.VMEM((h, kd, tm), F32)],
      compiler_params=_params(("arbitrary", "arbitrary")),
      name="peer_dense",
  )(xt, xsc, u, usc, v, s1, s2, stats)


def kernel(x, mem, ln_in_g, ln_in_b, w_in, ssd_conv_w, ssd_conv_b, ssd_dt_bias, ssd_a_log, ssd_d, ssd_norm_w, lru_conv_w, lru_conv_b, lru_w_a, lru_b_a, lru_w_i, lru_b_i, lru_lambda, w_proj_ssd, w_proj_lru, w_mix_out, ln1_g, ln1_b, xa_w_q, xa_w_k, xa_w_v, xa_w_o, ln2_g, ln2_b, peer_w_q, peer_keys_1, peer_keys_2, peer_u, peer_v, ln3_g, ln3_b):
  batch, seq, d = x.shape
  t = batch * seq
  assert w_in.shape[0] == 1, "single-layer trunk"

  w = w_in[0]
  c_dt = SSD_SLAB_COLS
  c_lru = c_dt + SSD_HEADS
  w_dt = jnp.pad(w[:, c_dt:c_lru], ((0, 0), (0, 128 - SSD_HEADS))).astype(BF16)

  h0, h0b, h0q, h0_scale = _ln_call(x.reshape(t, d), ln_in_g, ln_in_b)
  proj_ssd = _matmul_fp8(h0q, h0_scale, w[:, :c_dt], BF16, "in_proj_ssd")
  proj_lru = _matmul_fp8(h0q, h0_scale, w[:, c_lru:], BF16, "in_proj_lru")
  dt_raw = _matmul(h0b, w_dt, F32, "dt_proj")

  y_ssd = _ssd_call(proj_ssd, dt_raw, ssd_conv_w[0], ssd_conv_b[0], ssd_dt_bias[0], ssd_a_log[0],
                    ssd_d[0], ssd_norm_w[0], batch, seq)
  y_lru = _lru_call(proj_lru, lru_conv_w[0], lru_conv_b[0], lru_w_a[0], lru_b_a[0], lru_w_i[0],
                    lru_b_i[0], lru_lambda[0], batch, seq)
  merged = _merge_call(y_ssd, y_lru, w_proj_ssd[0].astype(BF16), w_proj_lru[0].astype(BF16),
                       proj_lru)
  mixed = _matmul(merged, w_mix_out[0].astype(BF16), F32, "mix_out")
  h1, h1b = _res_ln_call(h0, mixed, ln1_g[0], ln1_b[0], "res_ln1")

  mem_len = mem.shape[1]
  w_kv = jnp.concatenate([xa_w_k[0], xa_w_v[0]], axis=1).astype(BF16)
  kv = _matmul(mem.reshape(batch * mem_len, d).astype(BF16), w_kv, BF16, "mem_kv")
  xw = xa_w_k.shape[-1]
  k = kv[:, :xw].reshape(batch, mem_len, xw)
  v = kv[:, xw:].reshape(batch, mem_len, xw)
  h2, h2b, h2t8, h2t_scale = _xattn_call(h1, h1b, xa_w_q[0].astype(BF16), k, v,
                                         xa_w_o[0].astype(BF16), ln2_g[0], ln2_b[0], seq)

  q = _matmul(h2b, peer_w_q[0].astype(BF16), BF16, "peer_query")
  s1, s2, stats = _peer_route_call(q, peer_keys_1[0].astype(BF16), peer_keys_2[0].astype(BF16))
  u = peer_u[0]
  u_amax = jnp.maximum(jnp.max(jnp.abs(u), axis=1, keepdims=True), FP8_AMAX_FLOOR)
  u8 = (u * (FP8_TARGET / u_amax)).astype(FP8)
  u_scale = jnp.broadcast_to(u_amax * (1.0 / FP8_TARGET), (u.shape[0], 128))
  ffn = _peer_dense_call(h2t8, h2t_scale, u8, u_scale, peer_v[0].astype(BF16), s1, s2, stats)
  h3, _ = _res_ln_call(h2, ffn, ln3_g[0], ln3_b[0], "res_ln3")
  return h3.reshape(batch, seq, d)
```

```python
import functools

import jax
import jax.numpy as jnp
from jax import lax
from jax.experimental import pallas as pl
from jax.experimental.pallas import tpu as pltpu

F32 = jnp.float32
BF16 = jnp.bfloat16
FP8 = jnp.float8_e4m3fn
FP8_TARGET = 384.0
FP8_AMAX_FLOOR = 1e-30
FP8_ROUNDING_SLACK = 1.25

D_MODEL = 4096
SSD_HEADS = 64
SSD_HEAD_DIM = 64
SSD_GROUPS = 8
SSD_STATE = 128
SSD_CHUNK = 128
SSD_GROUP_WIDTH = D_MODEL // SSD_GROUPS
LRU_BLOCKS = 16
LRU_BLOCK_DIM = 256
LRU_C = 8.0
XATTN_HEADS = 4
XATTN_HEAD_DIM = 128
PEER_HEADS = 8
PEER_N_KEYS = 128
PEER_TOPK = 16
DEEPNORM_ALPHA = 2.0 ** 0.25
LN_EPS = 1e-5
RMS_EPS = 1e-6
CONV_TAPS = 4
HIST_ROWS = 8

VMEM_LIMIT_BYTES = 58 * 1024 * 1024

COL_Z, COL_XS, COL_BC = 0, 4096, 8192
SSD_SLAB_COLS = 10240
COL_LRU_GATE, COL_LRU_X, COL_GATE_SSD, COL_GATE_LRU = 0, 4096, 8192, 12288
BC_WIDTH = 2 * SSD_GROUPS * SSD_STATE


def _params(sem, flags=None):
  return pltpu.CompilerParams(dimension_semantics=sem, vmem_limit_bytes=VMEM_LIMIT_BYTES,
                              flags=flags)


def _layer_norm(x, g, b):
  mu = jnp.mean(x, axis=-1, keepdims=True)
  xc = x - mu
  var = jnp.mean(xc * xc, axis=-1, keepdims=True)
  return xc * lax.rsqrt(var + LN_EPS) * g + b


def _dot(a, b):
  return jnp.dot(a, b, preferred_element_type=F32)


def _dot_nt(a, b):
  return lax.dot_general(a, b, (((1,), (1,)), ((), ())), preferred_element_type=F32)


SCALE_LANES = 128


def _quantize_rows(y, q_ref, s_ref):
  amax = jnp.maximum(jnp.max(jnp.abs(y), axis=-1, keepdims=True), FP8_AMAX_FLOOR)
  q_ref[...] = (y * (FP8_TARGET / amax)).astype(FP8)
  s_ref[...] = jnp.broadcast_to(amax * (1.0 / FP8_TARGET), s_ref.shape)


def _ln_kernel(x_ref, g_ref, b_ref, o_ref, ob_ref, oq_ref, os_ref):
  y = _layer_norm(x_ref[...], g_ref[...], b_ref[...])
  o_ref[...] = y
  ob_ref[...] = y.astype(BF16)
  _quantize_rows(y, oq_ref, os_ref)


def _res_ln_kernel(h_ref, y_ref, g_ref, b_ref, o_ref, ob_ref):
  y = _layer_norm(DEEPNORM_ALPHA * h_ref[...] + y_ref[...], g_ref[...], b_ref[...])
  o_ref[...] = y
  ob_ref[...] = y.astype(BF16)


def _res_ln_scaled_kernel(h_ref, y_ref, s_ref, g_ref, b_ref, o_ref, ob_ref):
  y = y_ref[...] * pltpu.repeat(s_ref[...], y_ref.shape[1] // s_ref.shape[1], axis=1)
  out = _layer_norm(DEEPNORM_ALPHA * h_ref[...] + y, g_ref[...], b_ref[...])
  o_ref[...] = out
  ob_ref[...] = out.astype(BF16)


def _row_spec(bm, d):
  return pl.BlockSpec((bm, d), lambda i: (i, 0))


def _vec_spec(d):
  return pl.BlockSpec((1, d), lambda i: (0, 0))


def _ln_call(x, g, b, bm=256):
  t, d = x.shape
  return pl.pallas_call(
      _ln_kernel,
      grid=(t // bm,),
      in_specs=[_row_spec(bm, d), _vec_spec(d), _vec_spec(d)],
      out_specs=[_row_spec(bm, d), _row_spec(bm, d), _row_spec(bm, d),
                 _row_spec(bm, SCALE_LANES)],
      out_shape=[jax.ShapeDtypeStruct((t, d), F32), jax.ShapeDtypeStruct((t, d), BF16),
                 jax.ShapeDtypeStruct((t, d), FP8), jax.ShapeDtypeStruct((t, SCALE_LANES), F32)],
      compiler_params=_params(("arbitrary",)),
      name="ln_in",
  )(x, g.reshape(1, d), b.reshape(1, d))


def _res_ln_call(h, y, g, b, name, bm=256, y_scale=None):
  t, d = h.shape
  kernel_fn, extra, extra_specs = _res_ln_kernel, (), []
  if y_scale is not None:
    kernel_fn, extra, extra_specs = _res_ln_scaled_kernel, (y_scale,), [_row_spec(bm, SCALE_LANES)]
  return pl.pallas_call(
      kernel_fn,
      grid=(t // bm,),
      in_specs=[_row_spec(bm, d), _row_spec(bm, d)] + extra_specs + [_vec_spec(d), _vec_spec(d)],
      out_specs=[_row_spec(bm, d), _row_spec(bm, d)],
      out_shape=[jax.ShapeDtypeStruct((t, d), F32), jax.ShapeDtypeStruct((t, d), BF16)],
      compiler_params=_params(("arbitrary",)),
      name=name,
  )(h, y, *extra, g.reshape(1, d), b.reshape(1, d))


def _mm_kernel(x_ref, w_ref, o_ref):
  o_ref[...] = _dot(x_ref[...], w_ref[...]).astype(o_ref.dtype)


def _mm8_kernel(x_ref, xs_ref, w_ref, ws_ref, o_ref):
  acc = _dot(x_ref[...], w_ref[...])
  xs = pltpu.repeat(xs_ref[...], acc.shape[1] // xs_ref.shape[1], axis=1)
  o_ref[...] = (acc * xs * ws_ref[...]).astype(o_ref.dtype)


def _col_amax_kernel(w_ref, o_ref):
  @pl.when(pl.program_id(0) == 0)
  def _init():
    o_ref[...] = jnp.zeros(o_ref.shape, F32)

  bk, n = w_ref.shape
  x = jnp.abs(w_ref[...]).reshape(bk // HIST_ROWS, HIST_ROWS, n)
  o_ref[...] = jnp.maximum(o_ref[...], jnp.max(x, axis=0))


def _col_amax(w, bk=64):
  k, n = w.shape
  part = pl.pallas_call(
      _col_amax_kernel,
      grid=(k // bk,),
      in_specs=[pl.BlockSpec((bk, n), lambda i: (i, 0))],
      out_specs=pl.BlockSpec((HIST_ROWS, n), lambda i: (0, 0)),
      out_shape=jax.ShapeDtypeStruct((HIST_ROWS, n), F32),
      compiler_params=_params(("arbitrary",)),
      name="col_amax",
  )(w)
  return jnp.maximum(jnp.max(part, axis=0, keepdims=True), FP8_AMAX_FLOOR)


def _quantize_cols(w, w_amax):
  return (w * (FP8_TARGET / w_amax)).astype(FP8), w_amax * (1.0 / FP8_TARGET)


def _quantize_kernel(x_ref, q_ref, s_ref):
  _quantize_rows(x_ref[...].astype(F32), q_ref, s_ref)


def _quantize_call(x, name, bm=512):
  t, d = x.shape
  return pl.pallas_call(
      _quantize_kernel,
      grid=(t // bm,),
      in_specs=[_row_spec(bm, d)],
      out_specs=[_row_spec(bm, d), _row_spec(bm, SCALE_LANES)],
      out_shape=[jax.ShapeDtypeStruct((t, d), FP8), jax.ShapeDtypeStruct((t, SCALE_LANES), F32)],
      compiler_params=_params(("arbitrary",)),
      name=name,
  )(x)


def _matmul_fp8(x8, x_scale, w8, w_scale, out_dtype, name, bm=1024, bn=1024):
  m, k = x8.shape
  n = w8.shape[1]
  bm, bn = min(bm, m), min(bn, n)
  return pl.pallas_call(
      _mm8_kernel,
      grid=(m // bm, n // bn),
      in_specs=[pl.BlockSpec((bm, k), lambda i, j: (i, 0)),
                pl.BlockSpec((bm, x_scale.shape[1]), lambda i, j: (i, 0)),
                pl.BlockSpec((k, bn), lambda i, j: (0, j)),
                pl.BlockSpec((1, bn), lambda i, j: (0, j))],
      out_specs=pl.BlockSpec((bm, bn), lambda i, j: (i, j)),
      out_shape=jax.ShapeDtypeStruct((m, n), out_dtype),
      compiler_params=_params(("arbitrary", "arbitrary")),
      name=name,
  )(x8, x_scale, w8, w_scale)


def _matmul(x, w, out_dtype, name, bm=1024, bn=1024):
  m, k = x.shape
  n = w.shape[1]
  bm, bn = min(bm, m), min(bn, n)
  return pl.pallas_call(
      _mm_kernel,
      grid=(m // bm, n // bn),
      in_specs=[pl.BlockSpec((bm, k), lambda i, j: (i, 0)),
                pl.BlockSpec((k, bn), lambda i, j: (0, j))],
      out_specs=pl.BlockSpec((bm, bn), lambda i, j: (i, j)),
      out_shape=jax.ShapeDtypeStruct((m, n), out_dtype),
      compiler_params=_params(("arbitrary", "arbitrary")),
      name=name,
  )(x, w)


def _causal_conv(hist_ref, raw, w_ref, b_ref, cols=slice(None)):
  rows, c = raw.shape
  x3 = raw.astype(F32).reshape(rows // HIST_ROWS, HIST_ROWS, c)
  ext = jnp.concatenate([hist_ref[:, cols][None], x3], axis=0)
  hist_ref[:, cols] = x3[rows // HIST_ROWS - 1]
  sub = lax.broadcasted_iota(jnp.int32, x3.shape, 1)
  y = b_ref[:, cols] + w_ref[CONV_TAPS - 1:CONV_TAPS, cols] * x3
  for s in range(1, CONV_TAPS):
    rot = pltpu.roll(ext, s, axis=1)
    shifted = jnp.where(sub >= s, rot[1:], rot[:-1])
    y = y + w_ref[CONV_TAPS - 1 - s:CONV_TAPS - s, cols] * shifted
  return y


def _softplus(x):
  return jnp.maximum(x, 0.0) + jnp.log1p(jnp.exp(-jnp.abs(x)))


def _ssd_kernel(z_ref, x_ref, bc_ref, dt_ref, cwx_ref, cbx_ref, cwbc_ref, cbbc_ref,
                dtb_ref, alog_ref, dsk_ref, nw_ref, o_ref, xh_ref, bch_ref, st_ref):
  L = x_ref.shape[0]
  n = SSD_STATE
  gw = SSD_GROUP_WIDTH
  pair_w = 2 * SSD_HEAD_DIM
  hpg = SSD_HEADS // SSD_GROUPS

  @pl.when(pl.program_id(1) == 0)
  def _init():
    xh_ref[...] = jnp.zeros(xh_ref.shape, F32)
    bch_ref[...] = jnp.zeros(bch_ref.shape, F32)
    st_ref[...] = jnp.zeros(st_ref.shape, F32)

  dt = _softplus(dt_ref[...] + dtb_ref[...])
  acs = dt * (-jnp.exp(alog_ref[...]))
  row_t = lax.broadcasted_iota(jnp.int32, acs.shape, 0)
  shift = 1
  while shift < L:
    acs = acs + jnp.where(row_t >= shift, pltpu.roll(acs, shift, axis=0), 0.0)
    shift *= 2
  dt_t = dt.T
  acs_t = acs.T
  a_last = acs_t[:, L - 1:L]
  w_t = dt_t * jnp.exp(a_last - acs_t)
  dfs_all = jnp.exp(acs)
  cd_all = jnp.exp(a_last)

  row = lax.broadcasted_iota(jnp.int32, (L, L), 0)
  col = lax.broadcasted_iota(jnp.int32, (L, L), 1)
  causal = row >= col
  lo = lax.broadcasted_iota(jnp.int32, (L, pair_w), 1) < SSD_HEAD_DIM
  lo_n = lax.broadcasted_iota(jnp.int32, (n, pair_w), 1) < SSD_HEAD_DIM

  def conv_silu(hist_ref, raw_ref, w_ref, bias_ref, cols):
    y = _causal_conv(hist_ref, raw_ref[:, cols], w_ref, bias_ref, cols)
    y = y * jax.nn.sigmoid(y)
    return y.reshape(L, y.shape[2])

  for g in range(SSD_GROUPS):
    gcols = slice(g * gw, (g + 1) * gw)
    xs = conv_silu(xh_ref, x_ref, cwx_ref, cbx_ref, gcols)
    bm = conv_silu(bch_ref, bc_ref, cwbc_ref, cbbc_ref, slice(g * n, (g + 1) * n))
    cm = conv_silu(bch_ref, bc_ref, cwbc_ref, cbbc_ref,
                   slice((SSD_GROUPS + g) * n, (SSD_GROUPS + g + 1) * n))
    bm16 = bm.astype(BF16)
    cm16 = cm.astype(BF16)
    cb = _dot_nt(cm16, bm16)
    b_t = bm.T
    st = st_ref[g]
    y_off = _dot(cm16, st.astype(BF16))

    ys = []
    for q in range(gw // pair_w):
      sl = slice(q * pair_w, (q + 1) * pair_w)
      h0 = g * hpg + 2 * q
      xq = xs[:, sl]
      rhs = jnp.concatenate([jnp.where(lo, xq, 0.0), jnp.where(lo, 0.0, xq)],
                            axis=0).astype(BF16)
      m_parts, bw_parts = [], []
      for h in (h0, h0 + 1):
        seg = acs[:, h:h + 1] - acs_t[h:h + 1, :]
        lm = jnp.exp(jnp.where(causal, seg, -jnp.inf))
        m_parts.append((cb * lm * dt_t[h:h + 1, :]).astype(BF16))
        bw_parts.append((b_t * w_t[h:h + 1, :]).astype(BF16))
      lhs = jnp.concatenate([jnp.concatenate(m_parts, axis=1),
                             jnp.concatenate(bw_parts, axis=1)], axis=0)
      res = _dot(lhs, rhs)
      dfs = jnp.where(lo, dfs_all[:, h0:h0 + 1], dfs_all[:, h0 + 1:h0 + 2])
      ys.append(res[:L] + y_off[:, sl] * dfs)
      chunk_decay = jnp.where(lo_n, cd_all[h0:h0 + 1, :], cd_all[h0 + 1:h0 + 2, :])
      st_ref[g, :, sl] = st[:, sl] * chunk_decay + res[L:]

    y = jnp.concatenate(ys, axis=1) + dsk_ref[:, gcols] * xs
    zf = z_ref[:, gcols].astype(F32)
    y = y * (zf * jax.nn.sigmoid(zf))
    ms = jnp.mean(y * y, axis=-1, keepdims=True)
    o_ref[:, gcols] = (y * lax.rsqrt(ms + RMS_EPS) * nw_ref[:, gcols]).astype(BF16)


def _ssd_call(proj, dt_raw, conv_w, conv_b, dt_bias, a_log, d_skip, norm_w, batch, seq):
  t = proj.shape[0]
  L, n, d = SSD_CHUNK, SSD_STATE, D_MODEL
  nc = seq // L
  lanes = dt_raw.shape[1]
  pad = lanes - SSD_HEADS
  d_cols = jnp.repeat(d_skip, SSD_HEAD_DIM).reshape(1, d)

  def rows(b, c):
    return b * nc + c

  full = lambda r, c: pl.BlockSpec((r, c), lambda b, s: (0, 0))
  in_specs = [
      pl.BlockSpec((L, d), lambda b, c: (rows(b, c), COL_Z // d)),
      pl.BlockSpec((L, d), lambda b, c: (rows(b, c), COL_XS // d)),
      pl.BlockSpec((L, BC_WIDTH), lambda b, c: (rows(b, c), COL_BC // BC_WIDTH)),
      pl.BlockSpec((L, lanes), lambda b, c: (rows(b, c), 0)),
      full(CONV_TAPS, d), full(1, d), full(CONV_TAPS, BC_WIDTH), full(1, BC_WIDTH),
      full(1, lanes), full(1, lanes), full(1, d), full(1, d),
  ]
  return pl.pallas_call(
      _ssd_kernel,
      grid=(batch, nc),
      in_specs=in_specs,
      out_specs=pl.BlockSpec((L, d), lambda b, c: (rows(b, c), 0)),
      out_shape=jax.ShapeDtypeStruct((t, d), BF16),
      scratch_shapes=[
          pltpu.VMEM((HIST_ROWS, d), F32),
          pltpu.VMEM((HIST_ROWS, BC_WIDTH), F32),
          pltpu.VMEM((SSD_GROUPS, n, SSD_GROUP_WIDTH), F32),
      ],
      compiler_params=_params(("arbitrary", "arbitrary")),
      name="ssd_scan",
  )(proj, proj, proj, dt_raw, conv_w[:, :d], conv_b[:d].reshape(1, d), conv_w[:, d:],
    conv_b[d:].reshape(1, BC_WIDTH), jnp.pad(dt_bias, (0, pad)).reshape(1, lanes),
    jnp.pad(a_log, (0, pad)).reshape(1, lanes), d_cols, norm_w.reshape(1, d))


def _lru_kernel(x_ref, g_ref, cw_ref, cb_ref, wa_ref, ba_ref, wi_ref, bi_ref, lam_ref,
                o_ref, xh_ref, h_ref):
  ts, bd = x_ref.shape
  nt = ts // HIST_ROWS

  @pl.when(pl.program_id(2) == 0)
  def _init():
    xh_ref[...] = jnp.zeros(xh_ref.shape, F32)
    h_ref[...] = jnp.zeros(h_ref.shape, F32)

  xr3 = _causal_conv(xh_ref, x_ref[...], cw_ref, cb_ref)
  xr = xr3.reshape(ts, bd)
  xr16 = xr.astype(BF16)
  r = jax.nn.sigmoid(_dot(xr16, wa_ref[...]) + ba_ref[...])
  i = jax.nn.sigmoid(_dot(xr16, wi_ref[...]) + bi_ref[...])
  log_a = (-LRU_C) * r * _softplus(-lam_ref[...])
  a = jnp.exp(log_a)
  th = jnp.tanh(log_a)
  u = jnp.exp(0.5 * jnp.log(-2.0 * th / (1.0 - th))) * (i * xr)

  a = a.reshape(nt, HIST_ROWS, bd)
  u = u.reshape(nt, HIST_ROWS, bd)
  sub = lax.broadcasted_iota(jnp.int32, a.shape, 1)
  shift = 1
  while shift < HIST_ROWS:
    keep = sub >= shift
    a_prev = jnp.where(keep, pltpu.roll(a, shift, axis=1), 1.0)
    u_prev = jnp.where(keep, pltpu.roll(u, shift, axis=1), 0.0)
    u = a * u_prev + u
    a = a * a_prev
    shift *= 2
  carry = h_ref[...]
  hs = []
  for k in range(nt):
    hk = a[k] * carry + u[k]
    hs.append(hk)
    carry = jnp.broadcast_to(hk[HIST_ROWS - 1:HIST_ROWS, :], hk.shape)
  h_ref[...] = carry
  h = jnp.concatenate(hs, axis=0)
  o_ref[...] = (jax.nn.gelu(g_ref[...].astype(F32)) * h).astype(BF16)


def _lru_call(proj, conv_w, conv_b, w_a, b_a, w_i, b_i, lam, batch, seq, ts=512):
  t = proj.shape[0]
  bd = LRU_BLOCK_DIM
  nt = seq // ts

  def rows(b, k, s):
    return b * nt + s

  vec = lambda: pl.BlockSpec((1, bd), lambda b, k, s: (0, k))
  in_specs = [
      pl.BlockSpec((ts, bd), lambda b, k, s: (rows(b, k, s), COL_LRU_X // bd + k)),
      pl.BlockSpec((ts, bd), lambda b, k, s: (rows(b, k, s), COL_LRU_GATE // bd + k)),
      pl.BlockSpec((CONV_TAPS, bd), lambda b, k, s: (0, k)),
      vec(),
      pl.BlockSpec((None, bd, bd), lambda b, k, s: (k, 0, 0)),
      vec(),
      pl.BlockSpec((None, bd, bd), lambda b, k, s: (k, 0, 0)),
      vec(),
      vec(),
  ]
  return pl.pallas_call(
      _lru_kernel,
      grid=(batch, LRU_BLOCKS, nt),
      in_specs=in_specs,
      out_specs=pl.BlockSpec((ts, bd), lambda b, k, s: (rows(b, k, s), k)),
      out_shape=jax.ShapeDtypeStruct((t, D_MODEL), BF16),
      scratch_shapes=[pltpu.VMEM((HIST_ROWS, bd), F32), pltpu.VMEM((HIST_ROWS, bd), F32)],
      compiler_params=_params(("arbitrary", "arbitrary", "arbitrary")),
      name="rglru_scan",
  )(proj, proj, conv_w, conv_b.reshape(1, -1), w_a.astype(BF16), b_a.reshape(1, -1),
    w_i.astype(BF16), b_i.reshape(1, -1), lam.reshape(1, -1))


def _merge_kernel(ys_ref, yss_ref, yl_ref, yls_ref, ws_ref, wss_ref, wl_ref, wls_ref,
                  ga_ref, gb_ref, o_ref):
  bn = o_ref.shape[1]
  rep = bn // yss_ref.shape[1]
  pa = _dot(ys_ref[...], ws_ref[...]) * pltpu.repeat(yss_ref[...], rep, axis=1) * wss_ref[...]
  pb = _dot(yl_ref[...], wl_ref[...]) * pltpu.repeat(yls_ref[...], rep, axis=1) * wls_ref[...]
  ga = jax.nn.sigmoid(ga_ref[...].astype(F32))
  gb = jax.nn.sigmoid(gb_ref[...].astype(F32))
  o_ref[...] = (ga * pa + gb * pb).astype(o_ref.dtype)


def _merge_call(ys8, ys_scale, yl8, yl_scale, ws8, ws_scale, wl8, wl_scale, proj, bm=1024, bn=512):
  t, d = ys8.shape
  row = lambda w: pl.BlockSpec((bm, w), lambda i, j: (i, 0))
  col = lambda r: pl.BlockSpec((r, bn), lambda i, j: (0, j))
  return pl.pallas_call(
      _merge_kernel,
      grid=(t // bm, d // bn),
      in_specs=[
          row(d), row(SCALE_LANES), row(d), row(SCALE_LANES),
          col(d), col(1), col(d), col(1),
          pl.BlockSpec((bm, bn), lambda i, j: (i, COL_GATE_SSD // bn + j)),
          pl.BlockSpec((bm, bn), lambda i, j: (i, COL_GATE_LRU // bn + j)),
      ],
      out_specs=pl.BlockSpec((bm, bn), lambda i, j: (i, j)),
      out_shape=jax.ShapeDtypeStruct((t, d), BF16),
      compiler_params=_params(("arbitrary", "arbitrary")),
      name="gated_merge",
  )(ys8, ys_scale, yl8, yl_scale, ws8, ws_scale, wl8, wl_scale, proj, proj)


def _xattn_kernel(h_ref, hb_ref, wq_ref, k_ref, v_ref, wo_ref, g_ref, b_ref, o_ref, ob_ref,
                  obt_ref, osc_ref):
  hd = XATTN_HEAD_DIM
  q = _dot(hb_ref[...], wq_ref[...]).astype(BF16)
  k = k_ref[...]
  v = v_ref[...]
  outs = []
  for head in range(XATTN_HEADS):
    sl = slice(head * hd, (head + 1) * hd)
    s = _dot_nt(q[:, sl], k[:, sl]) * (hd ** -0.5)
    s = s - jnp.max(s, axis=-1, keepdims=True)
    p = jnp.exp(s)
    p = p / jnp.sum(p, axis=-1, keepdims=True)
    outs.append(_dot(p.astype(BF16), v[:, sl]))
  o = jnp.concatenate(outs, axis=1).astype(BF16)
  y = _dot(o, wo_ref[...])
  out = _layer_norm(DEEPNORM_ALPHA * h_ref[...] + y, g_ref[...], b_ref[...])
  o_ref[...] = out
  ob_ref[...] = out.astype(BF16)
  out_t = out.T
  amax = jnp.maximum(jnp.max(jnp.abs(out_t), axis=0, keepdims=True), FP8_AMAX_FLOOR)
  obt_ref[...] = (out_t * (FP8_TARGET / amax)).astype(FP8)
  osc_ref[0:1, :] = amax * (1.0 / FP8_TARGET)
  osc_ref[1:2, :] = jnp.sqrt(jnp.sum(out_t * out_t, axis=0, keepdims=True))


def _xattn_call(h, hb, w_q, k, v, w_o, g, b, seq, tm=256):
  t, d = h.shape
  xw = w_q.shape[1]
  mem_len = k.shape[1]
  per_batch = seq // tm
  return pl.pallas_call(
      _xattn_kernel,
      grid=(t // tm,),
      in_specs=[
          _row_spec(tm, d), _row_spec(tm, d),
          pl.BlockSpec((d, xw), lambda i: (0, 0)),
          pl.BlockSpec((None, mem_len, xw), lambda i: (i // per_batch, 0, 0)),
          pl.BlockSpec((None, mem_len, xw), lambda i: (i // per_batch, 0, 0)),
          pl.BlockSpec((xw, d), lambda i: (0, 0)),
          _vec_spec(d), _vec_spec(d),
      ],
      out_specs=[_row_spec(tm, d), _row_spec(tm, d), pl.BlockSpec((d, tm), lambda i: (0, i)),
                 pl.BlockSpec((2, tm), lambda i: (0, i))],
      out_shape=[jax.ShapeDtypeStruct((t, d), F32), jax.ShapeDtypeStruct((t, d), BF16),
                 jax.ShapeDtypeStruct((d, t), FP8), jax.ShapeDtypeStruct((2, t), F32)],
      compiler_params=_params(("arbitrary",)),
      name="mem_xattn_ln2",
  )(h, hb, w_q, k, v, w_o, g.reshape(1, d), b.reshape(1, d))


_PEER_CAND_PAIRS = tuple((a, b) for a in range(PEER_TOPK) for b in range(PEER_TOPK)
                         if (a + 1) * (b + 1) <= PEER_TOPK)
_PEER_CAND_ROWS = 56


def _peer_route_kernel(q_ref, k1_ref, k2_ref, s1_ref, s2_ref, st_ref, cand_ref):
  tm = q_ref.shape[0]
  kd = PEER_N_KEYS
  neg_inf = -jnp.inf

  def top_values(s):
    vals = []
    cur = s
    for _ in range(PEER_TOPK):
      m = jnp.max(cur, axis=0, keepdims=True)
      vals.append(m)
      cur = jnp.where(cur == m, neg_inf, cur)
    return vals

  cand_ref[...] = jnp.full(cand_ref.shape, neg_inf, F32)
  for head in range(PEER_HEADS):
    q1 = q_ref[:, head * 2 * kd:head * 2 * kd + kd]
    q2 = q_ref[:, head * 2 * kd + kd:(head + 1) * 2 * kd]
    s1 = _dot_nt(k1_ref[head], q1)
    s2 = _dot_nt(k2_ref[head], q2)
    s1_ref[head] = s1
    s2_ref[head] = s2
    v1 = top_values(s1)
    v2 = top_values(s2)
    for idx, (a, b) in enumerate(_PEER_CAND_PAIRS):
      cand_ref[idx:idx + 1, :] = v1[a] + v2[b]
    cand = cand_ref[...]
    cur = cand
    count = jnp.zeros((1, tm), F32)
    tau = jnp.full((1, tm), neg_inf, F32)
    for _ in range(PEER_TOPK):
      m = jnp.max(cur, axis=0, keepdims=True)
      eq = cur == m
      new_count = count + jnp.sum(jnp.where(eq, 1.0, 0.0), axis=0, keepdims=True)
      tau = jnp.where(count < PEER_TOPK, jnp.where(new_count >= PEER_TOPK, m, tau), tau)
      count = new_count
      cur = jnp.where(eq, neg_inf, cur)
    top = v1[0] + v2[0]
    z = jnp.sum(jnp.where(cand >= tau, jnp.exp(cand - top), 0.0), axis=0, keepdims=True)
    st_ref[0, head:head + 1, :] = tau
    st_ref[1, head:head + 1, :] = v1[0]
    st_ref[2, head:head + 1, :] = v2[0]
    st_ref[3, head:head + 1, :] = 1.0 / z


def _peer_route_call(q, keys_1, keys_2, tm=256):
  t = q.shape[0]
  h, kd = PEER_HEADS, PEER_N_KEYS
  score_spec = pl.BlockSpec((h, kd, tm), lambda i: (0, 0, i))
  return pl.pallas_call(
      _peer_route_kernel,
      grid=(t // tm,),
      in_specs=[pl.BlockSpec((tm, q.shape[1]), lambda i: (i, 0)),
                pl.BlockSpec((h, kd, kd), lambda i: (0, 0, 0)),
                pl.BlockSpec((h, kd, kd), lambda i: (0, 0, 0))],
      out_specs=[score_spec, score_spec, pl.BlockSpec((4, h, tm), lambda i: (0, 0, i))],
      out_shape=[jax.ShapeDtypeStruct((h, kd, t), F32), jax.ShapeDtypeStruct((h, kd, t), F32),
                 jax.ShapeDtypeStruct((4, h, t), F32)],
      scratch_shapes=[pltpu.VMEM((_PEER_CAND_ROWS, tm), F32)],
      compiler_params=_params(("arbitrary",)),
      name="peer_route",
  )(q, keys_1, keys_2)


def _peer_dense_kernel(xt_ref, tsc_ref, u_ref, usc_ref, v_ref, vsc_ref, s1_ref, s2_ref, st_ref,
                       o_ref, e2_ref):
  j = pl.program_id(1)
  tm = xt_ref.shape[1]
  half = u_ref.shape[0] // 2
  kd = PEER_N_KEYS
  rows_per_half = half // kd

  def gate_term(i1, head):
    s1_row = s1_ref[head, pl.ds(i1, 1), :]
    e1_row = jnp.exp(s1_row - st_ref[1, head:head + 1, :]) * st_ref[3, head:head + 1, :]
    sel = (s2_ref[head] + s1_row) >= st_ref[0, head:head + 1, :]
    return jnp.where(sel, e2_ref[head] * e1_row, 0.0)

  @pl.when(j == 0)
  def _init():
    o_ref[...] = jnp.zeros(o_ref.shape, F32)
    for head in range(PEER_HEADS):
      e2_ref[head] = jnp.exp(s2_ref[head] - st_ref[2, head:head + 1, :])

  def half_tile(half_index):
    rows = slice(half_index * half, (half_index + 1) * half)
    act = _dot(u_ref[rows, :], xt_ref[...])
    rep = tm // usc_ref.shape[1]
    act = act * pltpu.repeat(usc_ref[rows, :], rep, axis=1) * tsc_ref[0:1, :]
    gates = []
    for r in range(rows_per_half):
      i1 = (2 * j + half_index) * rows_per_half + r
      g = gate_term(i1, 0)
      for head in range(1, PEER_HEADS):
        g = g + gate_term(i1, head)
      gates.append(g)
    p = jnp.concatenate(gates, axis=0) * jax.nn.gelu(act)
    p = p * pltpu.repeat(vsc_ref[rows, :], rep, axis=1) * tsc_ref[1:2, :]
    p = p.T.astype(FP8)
    o_ref[...] += _dot(p, v_ref[rows, :])

  half_tile(0)
  half_tile(1)


def _peer_dense_call(xt, tsc, u, usc, v, vsc, s1, s2, stats, tm=512, te=512):
  d, t = xt.shape
  e = u.shape[0]
  h, kd = PEER_HEADS, PEER_N_KEYS
  score_spec = pl.BlockSpec((h, kd, tm), lambda i, j: (0, 0, i))
  return pl.pallas_call(
      _peer_dense_kernel,
      grid=(t // tm, e // te),
      in_specs=[pl.BlockSpec((d, tm), lambda i, j: (0, i)),
                pl.BlockSpec((2, tm), lambda i, j: (0, i)),
                pl.BlockSpec((te, d), lambda i, j: (j, 0)),
                pl.BlockSpec((te, SCALE_LANES), lambda i, j: (j, 0)),
                pl.BlockSpec((te, d), lambda i, j: (j, 0)),
                pl.BlockSpec((te, SCALE_LANES), lambda i, j: (j, 0)),
                score_spec, score_spec,
                pl.BlockSpec((4, h, tm), lambda i, j: (0, 0, i))],
      out_specs=pl.BlockSpec((tm, d), lambda i, j: (i, 0)),
      out_shape=jax.ShapeDtypeStruct((t, d), F32),
      scratch_shapes=[pltpu.VMEM((h, kd, tm), F32)],
      compiler_params=_params(("arbitrary", "arbitrary")),
      name="peer_dense",
  )(xt, tsc, u, usc, v, vsc, s1, s2, stats)


def kernel(x, mem, ln_in_g, ln_in_b, w_in, ssd_conv_w, ssd_conv_b, ssd_dt_bias, ssd_a_log, ssd_d, ssd_norm_w, lru_conv_w, lru_conv_b, lru_w_a, lru_b_a, lru_w_i, lru_b_i, lru_lambda, w_proj_ssd, w_proj_lru, w_mix_out, ln1_g, ln1_b, xa_w_q, xa_w_k, xa_w_v, xa_w_o, ln2_g, ln2_b, peer_w_q, peer_keys_1, peer_keys_2, peer_u, peer_v, ln3_g, ln3_b):
  batch, seq, d = x.shape
  t = batch * seq
  assert w_in.shape[0] == 1, "single-layer trunk"

  w = w_in[0]
  c_dt = SSD_SLAB_COLS
  c_lru = c_dt + SSD_HEADS
  w_dt = jnp.pad(w[:, c_dt:c_lru], ((0, 0), (0, 128 - SSD_HEADS))).astype(BF16)

  w_amax = _col_amax(w)
  w8, w_scale = _quantize_cols(w, w_amax)
  w_ssd8, w_ssd_scale = w8[:, :c_dt], w_scale[:, :c_dt]
  w_lru8, w_lru_scale = w8[:, c_lru:], w_scale[:, c_lru:]

  h0, h0b, h0q, h0_scale = _ln_call(x.reshape(t, d), ln_in_g, ln_in_b)
  proj_ssd = _matmul_fp8(h0q, h0_scale, w_ssd8, w_ssd_scale, BF16, "in_proj_ssd")
  proj_lru = _matmul_fp8(h0q, h0_scale, w_lru8, w_lru_scale, BF16, "in_proj_lru")
  dt_raw = _matmul(h0b, w_dt, F32, "dt_proj")

  y_ssd = _ssd_call(proj_ssd, dt_raw, ssd_conv_w[0], ssd_conv_b[0], ssd_dt_bias[0], ssd_a_log[0],
                    ssd_d[0], ssd_norm_w[0], batch, seq)
  y_lru = _lru_call(proj_lru, lru_conv_w[0], lru_conv_b[0], lru_w_a[0], lru_b_a[0], lru_w_i[0],
                    lru_b_i[0], lru_lambda[0], batch, seq)
  ys8, ys_scale = _quantize_call(y_ssd, "quant_y_ssd")
  yl8, yl_scale = _quantize_call(y_lru, "quant_y_lru")
  wps8, wps_scale = _quantize_cols(w_proj_ssd[0], _col_amax(w_proj_ssd[0]))
  wpl8, wpl_scale = _quantize_cols(w_proj_lru[0], _col_amax(w_proj_lru[0]))
  merged = _merge_call(ys8, ys_scale, yl8, yl_scale, wps8, wps_scale, wpl8, wpl_scale, proj_lru)
  mg8, mg_scale = _quantize_call(merged, "quant_merged")
  wmx8, wmx_scale = _quantize_cols(w_mix_out[0], _col_amax(w_mix_out[0]))
  mixed = _matmul_fp8(mg8, mg_scale, wmx8, wmx_scale, F32, "mix_out")
  h1, h1b = _res_ln_call(h0, mixed, ln1_g[0], ln1_b[0], "res_ln1")

  mem_len = mem.shape[1]
  w_kv = jnp.concatenate([xa_w_k[0], xa_w_v[0]], axis=1).astype(BF16)
  kv = _matmul(mem.reshape(batch * mem_len, d).astype(BF16), w_kv, BF16, "mem_kv")
  xw = xa_w_k.shape[-1]
  k = kv[:, :xw].reshape(batch, mem_len, xw)
  v = kv[:, xw:].reshape(batch, mem_len, xw)
  h2, h2b, h2t8, h2t_stats = _xattn_call(h1, h1b, xa_w_q[0].astype(BF16), k, v,
                                         xa_w_o[0].astype(BF16), ln2_g[0], ln2_b[0], seq)

  q = _matmul(h2b, peer_w_q[0].astype(BF16), BF16, "peer_query")
  s1, s2, stats = _peer_route_call(q, peer_keys_1[0].astype(BF16), peer_keys_2[0].astype(BF16))
  u, v_tab = peer_u[0], peer_v[0]
  n_exp = u.shape[0]
  u_amax = jnp.maximum(jnp.max(jnp.abs(u), axis=1, keepdims=True), FP8_AMAX_FLOOR)
  v_amax = jnp.maximum(jnp.max(jnp.abs(v_tab), axis=1, keepdims=True), FP8_AMAX_FLOOR)
  u8 = (u * (FP8_TARGET / u_amax)).astype(FP8)
  v8 = (v_tab * (FP8_TARGET / v_amax)).astype(FP8)
  u_scale = jnp.broadcast_to(u_amax * (1.0 / FP8_TARGET), (n_exp, SCALE_LANES))
  v_scale = v_amax * (1.0 / FP8_TARGET)
  u_norm = jnp.sqrt(jnp.sum(u * u, axis=1, keepdims=True))
  p_bound = (FP8_ROUNDING_SLACK * jnp.max(u_norm * v_scale)
             * jnp.sum(stats[3], axis=0, keepdims=True) * h2t_stats[1:2, :])
  p_bound = jnp.maximum(p_bound, FP8_AMAX_FLOOR)
  tok_scales = jnp.concatenate([h2t_stats[0:1, :], FP8_TARGET / p_bound], axis=0)
  ffn_scale = jnp.broadcast_to((p_bound * (1.0 / FP8_TARGET)).reshape(t, 1), (t, SCALE_LANES))
  ffn = _peer_dense_call(h2t8, tok_scales, u8, u_scale, v8,
                         jnp.broadcast_to(v_scale, (n_exp, SCALE_LANES)), s1, s2, stats)
  h3, _ = _res_ln_call(h2, ffn, ln3_g[0], ln3_b[0], "res_ln3", y_scale=ffn_scale)
  return h3.reshape(batch, seq, d)
```

```python
import functools

import jax
import jax.numpy as jnp
from jax import lax
from jax.experimental import pallas as pl
from jax.experimental.pallas import tpu as pltpu

F32 = jnp.float32
BF16 = jnp.bfloat16
FP8 = jnp.float8_e4m3fn
FP8_TARGET = 384.0
FP8_AMAX_FLOOR = 1e-30
FP8_ROUNDING_SLACK = 1.25

D_MODEL = 4096
SSD_HEADS = 64
SSD_HEAD_DIM = 64
SSD_GROUPS = 8
SSD_STATE = 128
SSD_CHUNK = 128
SSD_GROUP_WIDTH = D_MODEL // SSD_GROUPS
LRU_BLOCKS = 16
LRU_BLOCK_DIM = 256
LRU_C = 8.0
XATTN_HEADS = 4
XATTN_HEAD_DIM = 128
PEER_HEADS = 8
PEER_N_KEYS = 128
PEER_TOPK = 16
DEEPNORM_ALPHA = 2.0 ** 0.25
LN_EPS = 1e-5
RMS_EPS = 1e-6
CONV_TAPS = 4
HIST_ROWS = 8

VMEM_LIMIT_BYTES = 58 * 1024 * 1024

COL_Z, COL_XS, COL_BC = 0, 4096, 8192
SSD_SLAB_COLS = 10240
COL_LRU_GATE, COL_LRU_X, COL_GATE_SSD, COL_GATE_LRU = 0, 4096, 8192, 12288
BC_WIDTH = 2 * SSD_GROUPS * SSD_STATE


def _params(sem, flags=None):
  return pltpu.CompilerParams(dimension_semantics=sem, vmem_limit_bytes=VMEM_LIMIT_BYTES,
                              flags=flags)


def _layer_norm(x, g, b):
  mu = jnp.mean(x, axis=-1, keepdims=True)
  xc = x - mu
  var = jnp.mean(xc * xc, axis=-1, keepdims=True)
  return xc * lax.rsqrt(var + LN_EPS) * g + b


def _dot(a, b):
  return jnp.dot(a, b, preferred_element_type=F32)


def _dot_nt(a, b):
  return lax.dot_general(a, b, (((1,), (1,)), ((), ())), preferred_element_type=F32)


SCALE_LANES = 128


def _quantize_rows(y, q_ref, s_ref):
  amax = jnp.maximum(jnp.max(jnp.abs(y), axis=-1, keepdims=True), FP8_AMAX_FLOOR)
  q_ref[...] = (y * (FP8_TARGET / amax)).astype(FP8)
  s_ref[...] = jnp.broadcast_to(amax * (1.0 / FP8_TARGET), s_ref.shape)


def _ln_kernel(x_ref, g_ref, b_ref, o_ref, ob_ref, oq_ref, os_ref):
  y = _layer_norm(x_ref[...], g_ref[...], b_ref[...])
  o_ref[...] = y
  ob_ref[...] = y.astype(BF16)
  _quantize_rows(y, oq_ref, os_ref)


def _res_ln_kernel(h_ref, y_ref, g_ref, b_ref, o_ref, ob_ref):
  y = _layer_norm(DEEPNORM_ALPHA * h_ref[...] + y_ref[...], g_ref[...], b_ref[...])
  o_ref[...] = y
  ob_ref[...] = y.astype(BF16)


def _res_ln_scaled_kernel(h_ref, y_ref, s_ref, g_ref, b_ref, o_ref, ob_ref):
  y = y_ref[...] * pltpu.repeat(s_ref[...], y_ref.shape[1] // s_ref.shape[1], axis=1)
  out = _layer_norm(DEEPNORM_ALPHA * h_ref[...] + y, g_ref[...], b_ref[...])
  o_ref[...] = out
  ob_ref[...] = out.astype(BF16)


def _row_spec(bm, d):
  return pl.BlockSpec((bm, d), lambda i: (i, 0))


def _vec_spec(d):
  return pl.BlockSpec((1, d), lambda i: (0, 0))


def _ln_call(x, g, b, bm=256):
  t, d = x.shape
  return pl.pallas_call(
      _ln_kernel,
      grid=(t // bm,),
      in_specs=[_row_spec(bm, d), _vec_spec(d), _vec_spec(d)],
      out_specs=[_row_spec(bm, d), _row_spec(bm, d), _row_spec(bm, d),
                 _row_spec(bm, SCALE_LANES)],
      out_shape=[jax.ShapeDtypeStruct((t, d), F32), jax.ShapeDtypeStruct((t, d), BF16),
                 jax.ShapeDtypeStruct((t, d), FP8), jax.ShapeDtypeStruct((t, SCALE_LANES), F32)],
      compiler_params=_params(("arbitrary",)),
      name="ln_in",
  )(x, g.reshape(1, d), b.reshape(1, d))


def _res_ln_call(h, y, g, b, name, bm=256, y_scale=None):
  t, d = h.shape
  kernel_fn, extra, extra_specs = _res_ln_kernel, (), []
  if y_scale is not None:
    kernel_fn, extra, extra_specs = _res_ln_scaled_kernel, (y_scale,), [_row_spec(bm, SCALE_LANES)]
  return pl.pallas_call(
      kernel_fn,
      grid=(t // bm,),
      in_specs=[_row_spec(bm, d), _row_spec(bm, d)] + extra_specs + [_vec_spec(d), _vec_spec(d)],
      out_specs=[_row_spec(bm, d), _row_spec(bm, d)],
      out_shape=[jax.ShapeDtypeStruct((t, d), F32), jax.ShapeDtypeStruct((t, d), BF16)],
      compiler_params=_params(("arbitrary",)),
      name=name,
  )(h, y, *extra, g.reshape(1, d), b.reshape(1, d))


def _mm_kernel(x_ref, w_ref, o_ref):
  o_ref[...] = _dot(x_ref[...], w_ref[...]).astype(o_ref.dtype)


def _mm8_kernel(x_ref, xs_ref, w_ref, ws_ref, o_ref):
  acc = _dot(x_ref[...], w_ref[...])
  xs = pltpu.repeat(xs_ref[...], acc.shape[1] // xs_ref.shape[1], axis=1)
  o_ref[...] = (acc * xs * ws_ref[...]).astype(o_ref.dtype)


def _col_amax_kernel(w_ref, o_ref):
  @pl.when(pl.program_id(0) == 0)
  def _init():
    o_ref[...] = jnp.zeros(o_ref.shape, F32)

  bk, n = w_ref.shape
  x = jnp.abs(w_ref[...]).reshape(bk // HIST_ROWS, HIST_ROWS, n)
  o_ref[...] = jnp.maximum(o_ref[...], jnp.max(x, axis=0))


def _col_amax(w, bk=64):
  k, n = w.shape
  part = pl.pallas_call(
      _col_amax_kernel,
      grid=(k // bk,),
      in_specs=[pl.BlockSpec((bk, n), lambda i: (i, 0))],
      out_specs=pl.BlockSpec((HIST_ROWS, n), lambda i: (0, 0)),
      out_shape=jax.ShapeDtypeStruct((HIST_ROWS, n), F32),
      compiler_params=_params(("arbitrary",)),
      name="col_amax",
  )(w)
  return jnp.maximum(jnp.max(part, axis=0, keepdims=True), FP8_AMAX_FLOOR)


def _quantize_cols(w, w_amax):
  return (w * (FP8_TARGET / w_amax)).astype(FP8), w_amax * (1.0 / FP8_TARGET)


def _quantize_kernel(x_ref, q_ref, s_ref):
  _quantize_rows(x_ref[...].astype(F32), q_ref, s_ref)


def _quantize_call(x, name, bm=512):
  t, d = x.shape
  return pl.pallas_call(
      _quantize_kernel,
      grid=(t // bm,),
      in_specs=[_row_spec(bm, d)],
      out_specs=[_row_spec(bm, d), _row_spec(bm, SCALE_LANES)],
      out_shape=[jax.ShapeDtypeStruct((t, d), FP8), jax.ShapeDtypeStruct((t, SCALE_LANES), F32)],
      compiler_params=_params(("arbitrary",)),
      name=name,
  )(x)


def _matmul_fp8(x8, x_scale, w8, w_scale, out_dtype, name, bm=1024, bn=1024):
  m, k = x8.shape
  n = w8.shape[1]
  bm, bn = min(bm, m), min(bn, n)
  return pl.pallas_call(
      _mm8_kernel,
      grid=(m // bm, n // bn),
      in_specs=[pl.BlockSpec((bm, k), lambda i, j: (i, 0)),
                pl.BlockSpec((bm, x_scale.shape[1]), lambda i, j: (i, 0)),
                pl.BlockSpec((k, bn), lambda i, j: (0, j)),
                pl.BlockSpec((1, bn), lambda i, j: (0, j))],
      out_specs=pl.BlockSpec((bm, bn), lambda i, j: (i, j)),
      out_shape=jax.ShapeDtypeStruct((m, n), out_dtype),
      compiler_params=_params(("arbitrary", "arbitrary")),
      name=name,
  )(x8, x_scale, w8, w_scale)


def _matmul(x, w, out_dtype, name, bm=1024, bn=1024):
  m, k = x.shape
  n = w.shape[1]
  bm, bn = min(bm, m), min(bn, n)
  return pl.pallas_call(
      _mm_kernel,
      grid=(m // bm, n // bn),
      in_specs=[pl.BlockSpec((bm, k), lambda i, j: (i, 0)),
                pl.BlockSpec((k, bn), lambda i, j: (0, j))],
      out_specs=pl.BlockSpec((bm, bn), lambda i, j: (i, j)),
      out_shape=jax.ShapeDtypeStruct((m, n), out_dtype),
      compiler_params=_params(("arbitrary", "arbitrary")),
      name=name,
  )(x, w)


def _causal_conv(hist_ref, raw, w_ref, b_ref, cols=slice(None)):
  rows, c = raw.shape
  x3 = raw.astype(F32).reshape(rows // HIST_ROWS, HIST_ROWS, c)
  ext = jnp.concatenate([hist_ref[:, cols][None], x3], axis=0)
  hist_ref[:, cols] = x3[rows // HIST_ROWS - 1]
  sub = lax.broadcasted_iota(jnp.int32, x3.shape, 1)
  y = b_ref[:, cols] + w_ref[CONV_TAPS - 1:CONV_TAPS, cols] * x3
  for s in range(1, CONV_TAPS):
    rot = pltpu.roll(ext, s, axis=1)
    shifted = jnp.where(sub >= s, rot[1:], rot[:-1])
    y = y + w_ref[CONV_TAPS - 1 - s:CONV_TAPS - s, cols] * shifted
  return y


def _softplus(x):
  return jnp.maximum(x, 0.0) + jnp.log1p(jnp.exp(-jnp.abs(x)))


def _ssd_kernel(z_ref, x_ref, bc_ref, dt_ref, cwx_ref, cbx_ref, cwbc_ref, cbbc_ref,
                dtb_ref, alog_ref, dsk_ref, nw_ref, o_ref, xh_ref, bch_ref, st_ref):
  L = x_ref.shape[0]
  n = SSD_STATE
  gw = SSD_GROUP_WIDTH
  pair_w = 2 * SSD_HEAD_DIM
  hpg = SSD_HEADS // SSD_GROUPS

  @pl.when(pl.program_id(1) == 0)
  def _init():
    xh_ref[...] = jnp.zeros(xh_ref.shape, F32)
    bch_ref[...] = jnp.zeros(bch_ref.shape, F32)
    st_ref[...] = jnp.zeros(st_ref.shape, F32)

  dt = _softplus(dt_ref[...] + dtb_ref[...])
  acs = dt * (-jnp.exp(alog_ref[...]))
  row_t = lax.broadcasted_iota(jnp.int32, acs.shape, 0)
  shift = 1
  while shift < L:
    acs = acs + jnp.where(row_t >= shift, pltpu.roll(acs, shift, axis=0), 0.0)
    shift *= 2
  dt_t = dt.T
  acs_t = acs.T
  a_last = acs_t[:, L - 1:L]
  w_t = dt_t * jnp.exp(a_last - acs_t)
  dfs_all = jnp.exp(acs)
  cd_all = jnp.exp(a_last)

  row = lax.broadcasted_iota(jnp.int32, (L, L), 0)
  col = lax.broadcasted_iota(jnp.int32, (L, L), 1)
  causal = row >= col
  lo = lax.broadcasted_iota(jnp.int32, (L, pair_w), 1) < SSD_HEAD_DIM
  lo_n = lax.broadcasted_iota(jnp.int32, (n, pair_w), 1) < SSD_HEAD_DIM

  def conv_silu(hist_ref, raw_ref, w_ref, bias_ref, cols):
    y = _causal_conv(hist_ref, raw_ref[:, cols], w_ref, bias_ref, cols)
    y = y * jax.nn.sigmoid(y)
    return y.reshape(L, y.shape[2])

  for g in range(SSD_GROUPS):
    gcols = slice(g * gw, (g + 1) * gw)
    xs = conv_silu(xh_ref, x_ref, cwx_ref, cbx_ref, gcols)
    bm = conv_silu(bch_ref, bc_ref, cwbc_ref, cbbc_ref, slice(g * n, (g + 1) * n))
    cm = conv_silu(bch_ref, bc_ref, cwbc_ref, cbbc_ref,
                   slice((SSD_GROUPS + g) * n, (SSD_GROUPS + g + 1) * n))
    bm16 = bm.astype(BF16)
    cm16 = cm.astype(BF16)
    cb = _dot_nt(cm16, bm16)
    b_t = bm.T
    st = st_ref[g]
    y_off = _dot(cm16, st.astype(BF16))

    ys = []
    for q in range(gw // pair_w):
      sl = slice(q * pair_w, (q + 1) * pair_w)
      h0 = g * hpg + 2 * q
      xq = xs[:, sl]
      rhs = jnp.concatenate([jnp.where(lo, xq, 0.0), jnp.where(lo, 0.0, xq)],
                            axis=0).astype(BF16)
      m_parts, bw_parts = [], []
      for h in (h0, h0 + 1):
        seg = acs[:, h:h + 1] - acs_t[h:h + 1, :]
        lm = jnp.exp(jnp.where(causal, seg, -jnp.inf))
        m_parts.append((cb * lm * dt_t[h:h + 1, :]).astype(BF16))
        bw_parts.append((b_t * w_t[h:h + 1, :]).astype(BF16))
      lhs = jnp.concatenate([jnp.concatenate(m_parts, axis=1),
                             jnp.concatenate(bw_parts, axis=1)], axis=0)
      res = _dot(lhs, rhs)
      dfs = jnp.where(lo, dfs_all[:, h0:h0 + 1], dfs_all[:, h0 + 1:h0 + 2])
      ys.append(res[:L] + y_off[:, sl] * dfs)
      chunk_decay = jnp.where(lo_n, cd_all[h0:h0 + 1, :], cd_all[h0 + 1:h0 + 2, :])
      st_ref[g, :, sl] = st[:, sl] * chunk_decay + res[L:]

    y = jnp.concatenate(ys, axis=1) + dsk_ref[:, gcols] * xs
    zf = z_ref[:, gcols].astype(F32)
    y = y * (zf * jax.nn.sigmoid(zf))
    ms = jnp.mean(y * y, axis=-1, keepdims=True)
    o_ref[:, gcols] = (y * lax.rsqrt(ms + RMS_EPS) * nw_ref[:, gcols]).astype(BF16)


def _ssd_call(proj, dt_raw, conv_w, conv_b, dt_bias, a_log, d_skip, norm_w, batch, seq):
  t = proj.shape[0]
  L, n, d = SSD_CHUNK, SSD_STATE, D_MODEL
  nc = seq // L
  lanes = dt_raw.shape[1]
  pad = lanes - SSD_HEADS
  d_cols = jnp.repeat(d_skip, SSD_HEAD_DIM).reshape(1, d)

  def rows(b, c):
    return b * nc + c

  full = lambda r, c: pl.BlockSpec((r, c), lambda b, s: (0, 0))
  in_specs = [
      pl.BlockSpec((L, d), lambda b, c: (rows(b, c), COL_Z // d)),
      pl.BlockSpec((L, d), lambda b, c: (rows(b, c), COL_XS // d)),
      pl.BlockSpec((L, BC_WIDTH), lambda b, c: (rows(b, c), COL_BC // BC_WIDTH)),
      pl.BlockSpec((L, lanes), lambda b, c: (rows(b, c), 0)),
      full(CONV_TAPS, d), full(1, d), full(CONV_TAPS, BC_WIDTH), full(1, BC_WIDTH),
      full(1, lanes), full(1, lanes), full(1, d), full(1, d),
  ]
  return pl.pallas_call(
      _ssd_kernel,
      grid=(batch, nc),
      in_specs=in_specs,
      out_specs=pl.BlockSpec((L, d), lambda b, c: (rows(b, c), 0)),
      out_shape=jax.ShapeDtypeStruct((t, d), BF16),
      scratch_shapes=[
          pltpu.VMEM((HIST_ROWS, d), F32),
          pltpu.VMEM((HIST_ROWS, BC_WIDTH), F32),
          pltpu.VMEM((SSD_GROUPS, n, SSD_GROUP_WIDTH), F32),
      ],
      compiler_params=_params(("arbitrary", "arbitrary")),
      name="ssd_scan",
  )(proj, proj, proj, dt_raw, conv_w[:, :d], conv_b[:d].reshape(1, d), conv_w[:, d:],
    conv_b[d:].reshape(1, BC_WIDTH), jnp.pad(dt_bias, (0, pad)).reshape(1, lanes),
    jnp.pad(a_log, (0, pad)).reshape(1, lanes), d_cols, norm_w.reshape(1, d))


def _lru_kernel(x_ref, g_ref, cw_ref, cb_ref, wa_ref, ba_ref, wi_ref, bi_ref, lam_ref,
                o_ref, xh_ref, h_ref):
  ts, bd = x_ref.shape
  nt = ts // HIST_ROWS

  @pl.when(pl.program_id(2) == 0)
  def _init():
    xh_ref[...] = jnp.zeros(xh_ref.shape, F32)
    h_ref[...] = jnp.zeros(h_ref.shape, F32)

  xr3 = _causal_conv(xh_ref, x_ref[...], cw_ref, cb_ref)
  xr = xr3.reshape(ts, bd)
  xr16 = xr.astype(BF16)
  r = jax.nn.sigmoid(_dot(xr16, wa_ref[...]) + ba_ref[...])
  i = jax.nn.sigmoid(_dot(xr16, wi_ref[...]) + bi_ref[...])
  log_a = (-LRU_C) * r * _softplus(-lam_ref[...])
  a = jnp.exp(log_a)
  th = jnp.tanh(log_a)
  u = jnp.exp(0.5 * jnp.log(-2.0 * th / (1.0 - th))) * (i * xr)

  a = a.reshape(nt, HIST_ROWS, bd)
  u = u.reshape(nt, HIST_ROWS, bd)
  sub = lax.broadcasted_iota(jnp.int32, a.shape, 1)
  shift = 1
  while shift < HIST_ROWS:
    keep = sub >= shift
    a_prev = jnp.where(keep, pltpu.roll(a, shift, axis=1), 1.0)
    u_prev = jnp.where(keep, pltpu.roll(u, shift, axis=1), 0.0)
    u = a * u_prev + u
    a = a * a_prev
    shift *= 2
  carry = h_ref[...]
  hs = []
  for k in range(nt):
    hk = a[k] * carry + u[k]
    hs.append(hk)
    carry = jnp.broadcast_to(hk[HIST_ROWS - 1:HIST_ROWS, :], hk.shape)
  h_ref[...] = carry
  h = jnp.concatenate(hs, axis=0)
  o_ref[...] = (jax.nn.gelu(g_ref[...].astype(F32)) * h).astype(BF16)


def _lru_call(proj, conv_w, conv_b, w_a, b_a, w_i, b_i, lam, batch, seq, ts=512):
  t = proj.shape[0]
  bd = LRU_BLOCK_DIM
  nt = seq // ts

  def rows(b, k, s):
    return b * nt + s

  vec = lambda: pl.BlockSpec((1, bd), lambda b, k, s: (0, k))
  in_specs = [
      pl.BlockSpec((ts, bd), lambda b, k, s: (rows(b, k, s), COL_LRU_X // bd + k)),
      pl.BlockSpec((ts, bd), lambda b, k, s: (rows(b, k, s), COL_LRU_GATE // bd + k)),
      pl.BlockSpec((CONV_TAPS, bd), lambda b, k, s: (0, k)),
      vec(),
      pl.BlockSpec((None, bd, bd), lambda b, k, s: (k, 0, 0)),
      vec(),
      pl.BlockSpec((None, bd, bd), lambda b, k, s: (k, 0, 0)),
      vec(),
      vec(),
  ]
  return pl.pallas_call(
      _lru_kernel,
      grid=(batch, LRU_BLOCKS, nt),
      in_specs=in_specs,
      out_specs=pl.BlockSpec((ts, bd), lambda b, k, s: (rows(b, k, s), k)),
      out_shape=jax.ShapeDtypeStruct((t, D_MODEL), BF16),
      scratch_shapes=[pltpu.VMEM((HIST_ROWS, bd), F32), pltpu.VMEM((HIST_ROWS, bd), F32)],
      compiler_params=_params(("arbitrary", "arbitrary", "arbitrary")),
      name="rglru_scan",
  )(proj, proj, conv_w, conv_b.reshape(1, -1), w_a.astype(BF16), b_a.reshape(1, -1),
    w_i.astype(BF16), b_i.reshape(1, -1), lam.reshape(1, -1))


def _merge_kernel(ys_ref, yss_ref, yl_ref, yls_ref, ws_ref, wss_ref, wl_ref, wls_ref,
                  ga_ref, gb_ref, o_ref):
  bn = o_ref.shape[1]
  rep = bn // yss_ref.shape[1]
  pa = _dot(ys_ref[...], ws_ref[...]) * pltpu.repeat(yss_ref[...], rep, axis=1) * wss_ref[...]
  pb = _dot(yl_ref[...], wl_ref[...]) * pltpu.repeat(yls_ref[...], rep, axis=1) * wls_ref[...]
  ga = jax.nn.sigmoid(ga_ref[...].astype(F32))
  gb = jax.nn.sigmoid(gb_ref[...].astype(F32))
  o_ref[...] = (ga * pa + gb * pb).astype(o_ref.dtype)


def _merge_call(ys8, ys_scale, yl8, yl_scale, ws8, ws_scale, wl8, wl_scale, proj, bm=1024, bn=512):
  t, d = ys8.shape
  row = lambda w: pl.BlockSpec((bm, w), lambda i, j: (i, 0))
  col = lambda r: pl.BlockSpec((r, bn), lambda i, j: (0, j))
  return pl.pallas_call(
      _merge_kernel,
      grid=(t // bm, d // bn),
      in_specs=[
          row(d), row(SCALE_LANES), row(d), row(SCALE_LANES),
          col(d), col(1), col(d), col(1),
          pl.BlockSpec((bm, bn), lambda i, j: (i, COL_GATE_SSD // bn + j)),
          pl.BlockSpec((bm, bn), lambda i, j: (i, COL_GATE_LRU // bn + j)),
      ],
      out_specs=pl.BlockSpec((bm, bn), lambda i, j: (i, j)),
      out_shape=jax.ShapeDtypeStruct((t, d), BF16),
      compiler_params=_params(("arbitrary", "arbitrary")),
      name="gated_merge",
  )(ys8, ys_scale, yl8, yl_scale, ws8, ws_scale, wl8, wl_scale, proj, proj)


def _xattn_kernel(h_ref, hb_ref, wq_ref, k_ref, v_ref, wo_ref, g_ref, b_ref, o_ref, ob_ref,
                  obt_ref, osc_ref):
  hd = XATTN_HEAD_DIM
  q = _dot(hb_ref[...], wq_ref[...]).astype(BF16)
  k = k_ref[...]
  v = v_ref[...]
  outs = []
  for head in range(XATTN_HEADS):
    sl = slice(head * hd, (head + 1) * hd)
    s = _dot_nt(q[:, sl], k[:, sl]) * (hd ** -0.5)
    s = s - jnp.max(s, axis=-1, keepdims=True)
    p = jnp.exp(s)
    p = p / jnp.sum(p, axis=-1, keepdims=True)
    outs.append(_dot(p.astype(BF16), v[:, sl]))
  o = jnp.concatenate(outs, axis=1).astype(BF16)
  y = _dot(o, wo_ref[...])
  out = _layer_norm(DEEPNORM_ALPHA * h_ref[...] + y, g_ref[...], b_ref[...])
  o_ref[...] = out
  ob_ref[...] = out.astype(BF16)
  out_t = out.T
  amax = jnp.maximum(jnp.max(jnp.abs(out_t), axis=0, keepdims=True), FP8_AMAX_FLOOR)
  obt_ref[...] = (out_t * (FP8_TARGET / amax)).astype(FP8)
  osc_ref[0:1, :] = amax * (1.0 / FP8_TARGET)
  osc_ref[1:2, :] = jnp.sqrt(jnp.sum(out_t * out_t, axis=0, keepdims=True))


def _xattn_call(h, hb, w_q, k, v, w_o, g, b, seq, tm=256):
  t, d = h.shape
  xw = w_q.shape[1]
  mem_len = k.shape[1]
  per_batch = seq // tm
  return pl.pallas_call(
      _xattn_kernel,
      grid=(t // tm,),
      in_specs=[
          _row_spec(tm, d), _row_spec(tm, d),
          pl.BlockSpec((d, xw), lambda i: (0, 0)),
          pl.BlockSpec((None, mem_len, xw), lambda i: (i // per_batch, 0, 0)),
          pl.BlockSpec((None, mem_len, xw), lambda i: (i // per_batch, 0, 0)),
          pl.BlockSpec((xw, d), lambda i: (0, 0)),
          _vec_spec(d), _vec_spec(d),
      ],
      out_specs=[_row_spec(tm, d), _row_spec(tm, d), pl.BlockSpec((d, tm), lambda i: (0, i)),
                 pl.BlockSpec((2, tm), lambda i: (0, i))],
      out_shape=[jax.ShapeDtypeStruct((t, d), F32), jax.ShapeDtypeStruct((t, d), BF16),
                 jax.ShapeDtypeStruct((d, t), FP8), jax.ShapeDtypeStruct((2, t), F32)],
      compiler_params=_params(("arbitrary",)),
      name="mem_xattn_ln2",
  )(h, hb, w_q, k, v, w_o, g.reshape(1, d), b.reshape(1, d))


_PEER_CAND_PAIRS = tuple((a, b) for a in range(PEER_TOPK) for b in range(PEER_TOPK)
                         if (a + 1) * (b + 1) <= PEER_TOPK)
_PEER_CAND_ROWS = 56


def _peer_route_kernel(q_ref, k1_ref, k2_ref, s1_ref, s2_ref, st_ref, cand_ref):
  tm = q_ref.shape[0]
  kd = PEER_N_KEYS
  neg_inf = -jnp.inf

  def top_values(s):
    vals = []
    cur = s
    for _ in range(PEER_TOPK):
      m = jnp.max(cur, axis=0, keepdims=True)
      vals.append(m)
      cur = jnp.where(cur == m, neg_inf, cur)
    return vals

  cand_ref[...] = jnp.full(cand_ref.shape, neg_inf, F32)
  for head in range(PEER_HEADS):
    q1 = q_ref[:, head * 2 * kd:head * 2 * kd + kd]
    q2 = q_ref[:, head * 2 * kd + kd:(head + 1) * 2 * kd]
    s1 = _dot_nt(k1_ref[head], q1)
    s2 = _dot_nt(k2_ref[head], q2)
    s1_ref[head] = s1
    s2_ref[head] = s2
    v1 = top_values(s1)
    v2 = top_values(s2)
    for idx, (a, b) in enumerate(_PEER_CAND_PAIRS):
      cand_ref[idx:idx + 1, :] = v1[a] + v2[b]
    cand = cand_ref[...]
    cur = cand
    count = jnp.zeros((1, tm), F32)
    tau = jnp.full((1, tm), neg_inf, F32)
    for _ in range(PEER_TOPK):
      m = jnp.max(cur, axis=0, keepdims=True)
      eq = cur == m
      new_count = count + jnp.sum(jnp.where(eq, 1.0, 0.0), axis=0, keepdims=True)
      tau = jnp.where(count < PEER_TOPK, jnp.where(new_count >= PEER_TOPK, m, tau), tau)
      count = new_count
      cur = jnp.where(eq, neg_inf, cur)
    top = v1[0] + v2[0]
    z = jnp.sum(jnp.where(cand >= tau, jnp.exp(cand - top), 0.0), axis=0, keepdims=True)
    st_ref[0, head:head + 1, :] = tau
    st_ref[1, head:head + 1, :] = v1[0]
    st_ref[2, head:head + 1, :] = v2[0]
    st_ref[3, head:head + 1, :] = 1.0 / z


def _peer_route_call(q, keys_1, keys_2, tm=256):
  t = q.shape[0]
  h, kd = PEER_HEADS, PEER_N_KEYS
  score_spec = pl.BlockSpec((h, kd, tm), lambda i: (0, 0, i))
  return pl.pallas_call(
      _peer_route_kernel,
      grid=(t // tm,),
      in_specs=[pl.BlockSpec((tm, q.shape[1]), lambda i: (i, 0)),
                pl.BlockSpec((h, kd, kd), lambda i: (0, 0, 0)),
                pl.BlockSpec((h, kd, kd), lambda i: (0, 0, 0))],
      out_specs=[score_spec, score_spec, pl.BlockSpec((4, h, tm), lambda i: (0, 0, i))],
      out_shape=[jax.ShapeDtypeStruct((h, kd, t), F32), jax.ShapeDtypeStruct((h, kd, t), F32),
                 jax.ShapeDtypeStruct((4, h, t), F32)],
      scratch_shapes=[pltpu.VMEM((_PEER_CAND_ROWS, tm), F32)],
      compiler_params=_params(("arbitrary",)),
      name="peer_route",
  )(q, keys_1, keys_2)


BF16_ROWS = 16


def _peer_dense_kernel(xt_ref, tsc_ref, u_ref, v_ref, s1_ref, s2_ref, st_ref, o_ref,
                       d2_ref, e2_ref):
  j = pl.program_id(1)
  tm = xt_ref.shape[1]
  half = u_ref.shape[0] // 2
  kd = PEER_N_KEYS
  rows_per_half = half // kd
  tiles = kd // BF16_ROWS

  def packed(x):
    return x.astype(BF16).reshape(tiles, BF16_ROWS, tm)

  def packed_row(row):
    return jnp.broadcast_to(row, (BF16_ROWS, tm)).astype(BF16)[None]

  @pl.when(j == 0)
  def _init():
    o_ref[...] = jnp.zeros(o_ref.shape, F32)
    for head in range(PEER_HEADS):
      d2_ref[head] = packed(s2_ref[head] - st_ref[0, head:head + 1, :])
      e2_ref[head] = packed(jnp.exp(s2_ref[head] - st_ref[2, head:head + 1, :]))

  def gate_term(i1, head):
    s1_row = s1_ref[head, pl.ds(i1, 1), :]
    e1_row = (jnp.exp(s1_row - st_ref[1, head:head + 1, :]) * st_ref[3, head:head + 1, :]
              * tsc_ref[1:2, :])
    sel = d2_ref[head] >= packed_row(-s1_row)
    return jnp.where(sel, e2_ref[head] * packed_row(e1_row), jnp.zeros((), BF16))

  def half_tile(half_index):
    rows = slice(half_index * half, (half_index + 1) * half)
    act = _dot(u_ref[rows, :], xt_ref[...]) * tsc_ref[0:1, :]
    gates = []
    for r in range(rows_per_half):
      i1 = (2 * j + half_index) * rows_per_half + r
      g = gate_term(i1, 0)
      for head in range(1, PEER_HEADS):
        g = g + gate_term(i1, head)
      gates.append(g.reshape(kd, tm))
    p = jnp.concatenate(gates, axis=0).astype(F32) * jax.nn.gelu(act)
    o_ref[...] += _dot(p.T.astype(FP8), v_ref[rows, :])

  half_tile(0)
  half_tile(1)


def _peer_dense_call(xt, tsc, u, v, s1, s2, stats, tm=512, te=512):
  d, t = xt.shape
  e = u.shape[0]
  h, kd = PEER_HEADS, PEER_N_KEYS
  score_spec = pl.BlockSpec((h, kd, tm), lambda i, j: (0, 0, i))
  packed = pltpu.VMEM((h, kd // BF16_ROWS, BF16_ROWS, tm), BF16)
  return pl.pallas_call(
      _peer_dense_kernel,
      grid=(t // tm, e // te),
      in_specs=[pl.BlockSpec((d, tm), lambda i, j: (0, i)),
                pl.BlockSpec((2, tm), lambda i, j: (0, i)),
                pl.BlockSpec((te, d), lambda i, j: (j, 0)),
                pl.BlockSpec((te, d), lambda i, j: (j, 0)),
                score_spec, score_spec,
                pl.BlockSpec((4, h, tm), lambda i, j: (0, 0, i))],
      out_specs=pl.BlockSpec((tm, d), lambda i, j: (i, 0)),
      out_shape=jax.ShapeDtypeStruct((t, d), F32),
      scratch_shapes=[packed, packed],
      compiler_params=_params(("arbitrary", "arbitrary")),
      name="peer_dense",
  )(xt, tsc, u, v, s1, s2, stats)


def kernel(x, mem, ln_in_g, ln_in_b, w_in, ssd_conv_w, ssd_conv_b, ssd_dt_bias, ssd_a_log, ssd_d, ssd_norm_w, lru_conv_w, lru_conv_b, lru_w_a, lru_b_a, lru_w_i, lru_b_i, lru_lambda, w_proj_ssd, w_proj_lru, w_mix_out, ln1_g, ln1_b, xa_w_q, xa_w_k, xa_w_v, xa_w_o, ln2_g, ln2_b, peer_w_q, peer_keys_1, peer_keys_2, peer_u, peer_v, ln3_g, ln3_b):
  batch, seq, d = x.shape
  t = batch * seq
  assert w_in.shape[0] == 1, "single-layer trunk"

  w = w_in[0]
  c_dt = SSD_SLAB_COLS
  c_lru = c_dt + SSD_HEADS
  w_dt = jnp.pad(w[:, c_dt:c_lru], ((0, 0), (0, 128 - SSD_HEADS))).astype(BF16)

  w_amax = _col_amax(w)
  w8, w_scale = _quantize_cols(w, w_amax)
  w_ssd8, w_ssd_scale = w8[:, :c_dt], w_scale[:, :c_dt]
  w_lru8, w_lru_scale = w8[:, c_lru:], w_scale[:, c_lru:]

  h0, h0b, h0q, h0_scale = _ln_call(x.reshape(t, d), ln_in_g, ln_in_b)
  proj_ssd = _matmul_fp8(h0q, h0_scale, w_ssd8, w_ssd_scale, BF16, "in_proj_ssd")
  proj_lru = _matmul_fp8(h0q, h0_scale, w_lru8, w_lru_scale, BF16, "in_proj_lru")
  dt_raw = _matmul(h0b, w_dt, F32, "dt_proj")

  y_ssd = _ssd_call(proj_ssd, dt_raw, ssd_conv_w[0], ssd_conv_b[0], ssd_dt_bias[0], ssd_a_log[0],
                    ssd_d[0], ssd_norm_w[0], batch, seq)
  y_lru = _lru_call(proj_lru, lru_conv_w[0], lru_conv_b[0], lru_w_a[0], lru_b_a[0], lru_w_i[0],
                    lru_b_i[0], lru_lambda[0], batch, seq)
  ys8, ys_scale = _quantize_call(y_ssd, "quant_y_ssd")
  yl8, yl_scale = _quantize_call(y_lru, "quant_y_lru")
  wps8, wps_scale = _quantize_cols(w_proj_ssd[0], _col_amax(w_proj_ssd[0]))
  wpl8, wpl_scale = _quantize_cols(w_proj_lru[0], _col_amax(w_proj_lru[0]))
  merged = _merge_call(ys8, ys_scale, yl8, yl_scale, wps8, wps_scale, wpl8, wpl_scale, proj_lru)
  mg8, mg_scale = _quantize_call(merged, "quant_merged")
  wmx8, wmx_scale = _quantize_cols(w_mix_out[0], _col_amax(w_mix_out[0]))
  mixed = _matmul_fp8(mg8, mg_scale, wmx8, wmx_scale, F32, "mix_out")
  h1, h1b = _res_ln_call(h0, mixed, ln1_g[0], ln1_b[0], "res_ln1")

  mem_len = mem.shape[1]
  w_kv = jnp.concatenate([xa_w_k[0], xa_w_v[0]], axis=1).astype(BF16)
  kv = _matmul(mem.reshape(batch * mem_len, d).astype(BF16), w_kv, BF16, "mem_kv")
  xw = xa_w_k.shape[-1]
  k = kv[:, :xw].reshape(batch, mem_len, xw)
  v = kv[:, xw:].reshape(batch, mem_len, xw)
  h2, h2b, h2t8, h2t_stats = _xattn_call(h1, h1b, xa_w_q[0].astype(BF16), k, v,
                                         xa_w_o[0].astype(BF16), ln2_g[0], ln2_b[0], seq)

  q = _matmul(h2b, peer_w_q[0].astype(BF16), BF16, "peer_query")
  s1, s2, stats = _peer_route_call(q, peer_keys_1[0].astype(BF16), peer_keys_2[0].astype(BF16))
  u, v_tab = peer_u[0], peer_v[0]
  u_amax = jnp.maximum(jnp.max(jnp.abs(u)), FP8_AMAX_FLOOR)
  v_amax = jnp.maximum(jnp.max(jnp.abs(v_tab)), FP8_AMAX_FLOOR)
  u8 = (u * (FP8_TARGET / u_amax)).astype(FP8)
  v8 = (v_tab * (FP8_TARGET / v_amax)).astype(FP8)
  u_norm_max = jnp.sqrt(jnp.max(jnp.sum(u * u, axis=1)))
  p_bound = (FP8_ROUNDING_SLACK * u_norm_max
             * jnp.sum(stats[3], axis=0, keepdims=True) * h2t_stats[1:2, :])
  p_bound = jnp.maximum(p_bound, FP8_AMAX_FLOOR)
  tok_scales = jnp.concatenate([h2t_stats[0:1, :] * (u_amax * (1.0 / FP8_TARGET)),
                                FP8_TARGET / p_bound], axis=0)
  ffn_scale = jnp.broadcast_to(
      (p_bound * (v_amax * (1.0 / FP8_TARGET) ** 2)).reshape(t, 1), (t, SCALE_LANES))
  ffn = _peer_dense_call(h2t8, tok_scales, u8, v8, s1, s2, stats)
  h3, _ = _res_ln_call(h2, ffn, ln3_g[0], ln3_b[0], "res_ln3", y_scale=ffn_scale)
  return h3.reshape(batch, seq, d)
```

```python
import functools

import jax
import jax.numpy as jnp
from jax import lax
from jax.experimental import pallas as pl
from jax.experimental.pallas import tpu as pltpu

F32 = jnp.float32
BF16 = jnp.bfloat16
FP8 = jnp.float8_e4m3fn
FP8_TARGET = 384.0
FP8_AMAX_FLOOR = 1e-30
FP8_ROUNDING_SLACK = 1.25

D_MODEL = 4096
SSD_HEADS = 64
SSD_HEAD_DIM = 64
SSD_GROUPS = 8
SSD_STATE = 128
SSD_CHUNK = 128
SSD_GROUP_WIDTH = D_MODEL // SSD_GROUPS
LRU_BLOCKS = 16
LRU_BLOCK_DIM = 256
LRU_C = 8.0
XATTN_HEADS = 4
XATTN_HEAD_DIM = 128
PEER_HEADS = 8
PEER_N_KEYS = 128
PEER_TOPK = 16
DEEPNORM_ALPHA = 2.0 ** 0.25
LN_EPS = 1e-5
RMS_EPS = 1e-6
CONV_TAPS = 4
HIST_ROWS = 8

VMEM_LIMIT_BYTES = 58 * 1024 * 1024

COL_Z, COL_XS, COL_BC = 0, 4096, 8192
SSD_SLAB_COLS = 10240
COL_LRU_GATE, COL_LRU_X, COL_GATE_SSD, COL_GATE_LRU = 0, 4096, 8192, 12288
BC_WIDTH = 2 * SSD_GROUPS * SSD_STATE


def _params(sem, flags=None):
  return pltpu.CompilerParams(dimension_semantics=sem, vmem_limit_bytes=VMEM_LIMIT_BYTES,
                              flags=flags)


def _layer_norm(x, g, b):
  mu = jnp.mean(x, axis=-1, keepdims=True)
  xc = x - mu
  var = jnp.mean(xc * xc, axis=-1, keepdims=True)
  return xc * lax.rsqrt(var + LN_EPS) * g + b


def _dot(a, b):
  return jnp.dot(a, b, preferred_element_type=F32)


def _dot_nt(a, b):
  return lax.dot_general(a, b, (((1,), (1,)), ((), ())), preferred_element_type=F32)


SCALE_LANES = 128


def _quantize_rows(y, q_ref, s_ref):
  amax = jnp.maximum(jnp.max(jnp.abs(y), axis=-1, keepdims=True), FP8_AMAX_FLOOR)
  q_ref[...] = (y * (FP8_TARGET / amax)).astype(FP8)
  s_ref[...] = jnp.broadcast_to(amax * (1.0 / FP8_TARGET), s_ref.shape)


def _ln_kernel(x_ref, g_ref, b_ref, o_ref, ob_ref, oq_ref, os_ref):
  y = _layer_norm(x_ref[...], g_ref[...], b_ref[...])
  o_ref[...] = y
  ob_ref[...] = y.astype(BF16)
  _quantize_rows(y, oq_ref, os_ref)


def _res_ln_kernel(h_ref, y_ref, g_ref, b_ref, o_ref, ob_ref):
  y = _layer_norm(DEEPNORM_ALPHA * h_ref[...] + y_ref[...], g_ref[...], b_ref[...])
  o_ref[...] = y
  ob_ref[...] = y.astype(BF16)


def _res_ln_scaled_kernel(h_ref, y_ref, s_ref, g_ref, b_ref, o_ref, ob_ref):
  y = y_ref[...] * pltpu.repeat(s_ref[...], y_ref.shape[1] // s_ref.shape[1], axis=1)
  out = _layer_norm(DEEPNORM_ALPHA * h_ref[...] + y, g_ref[...], b_ref[...])
  o_ref[...] = out
  ob_ref[...] = out.astype(BF16)


def _row_spec(bm, d):
  return pl.BlockSpec((bm, d), lambda i: (i, 0))


def _vec_spec(d):
  return pl.BlockSpec((1, d), lambda i: (0, 0))


def _ln_call(x, g, b, bm=256):
  t, d = x.shape
  return pl.pallas_call(
      _ln_kernel,
      grid=(t // bm,),
      in_specs=[_row_spec(bm, d), _vec_spec(d), _vec_spec(d)],
      out_specs=[_row_spec(bm, d), _row_spec(bm, d), _row_spec(bm, d),
                 _row_spec(bm, SCALE_LANES)],
      out_shape=[jax.ShapeDtypeStruct((t, d), F32), jax.ShapeDtypeStruct((t, d), BF16),
                 jax.ShapeDtypeStruct((t, d), FP8), jax.ShapeDtypeStruct((t, SCALE_LANES), F32)],
      compiler_params=_params(("arbitrary",)),
      name="ln_in",
  )(x, g.reshape(1, d), b.reshape(1, d))


def _res_ln_call(h, y, g, b, name, bm=256, y_scale=None):
  t, d = h.shape
  kernel_fn, extra, extra_specs = _res_ln_kernel, (), []
  if y_scale is not None:
    kernel_fn, extra, extra_specs = _res_ln_scaled_kernel, (y_scale,), [_row_spec(bm, SCALE_LANES)]
  return pl.pallas_call(
      kernel_fn,
      grid=(t // bm,),
      in_specs=[_row_spec(bm, d), _row_spec(bm, d)] + extra_specs + [_vec_spec(d), _vec_spec(d)],
      out_specs=[_row_spec(bm, d), _row_spec(bm, d)],
      out_shape=[jax.ShapeDtypeStruct((t, d), F32), jax.ShapeDtypeStruct((t, d), BF16)],
      compiler_params=_params(("arbitrary",)),
      name=name,
  )(h, y, *extra, g.reshape(1, d), b.reshape(1, d))


def _mm_kernel(x_ref, w_ref, o_ref):
  o_ref[...] = _dot(x_ref[...], w_ref[...]).astype(o_ref.dtype)


def _mm8_kernel(x_ref, xs_ref, w_ref, ws_ref, o_ref):
  acc = _dot(x_ref[...], w_ref[...])
  xs = pltpu.repeat(xs_ref[...], acc.shape[1] // xs_ref.shape[1], axis=1)
  o_ref[...] = (acc * xs * ws_ref[...]).astype(o_ref.dtype)


def _col_amax_kernel(w_ref, o_ref):
  @pl.when(pl.program_id(0) == 0)
  def _init():
    o_ref[...] = jnp.zeros(o_ref.shape, F32)

  bk, n = w_ref.shape
  x = jnp.abs(w_ref[...]).reshape(bk // HIST_ROWS, HIST_ROWS, n)
  o_ref[...] = jnp.maximum(o_ref[...], jnp.max(x, axis=0))


def _col_amax(w):
  k, n = w.shape
  bk = 512 if n <= D_MODEL else 64
  part = pl.pallas_call(
      _col_amax_kernel,
      grid=(k // bk,),
      in_specs=[pl.BlockSpec((bk, n), lambda i: (i, 0))],
      out_specs=pl.BlockSpec((HIST_ROWS, n), lambda i: (0, 0)),
      out_shape=jax.ShapeDtypeStruct((HIST_ROWS, n), F32),
      compiler_params=_params(("arbitrary",)),
      name="col_amax",
  )(w)
  return jnp.maximum(jnp.max(part, axis=0, keepdims=True), FP8_AMAX_FLOOR)


def _quantize_cols(w, w_amax):
  return (w * (FP8_TARGET / w_amax)).astype(FP8), w_amax * (1.0 / FP8_TARGET)


def _quantize_kernel(x_ref, q_ref, s_ref):
  _quantize_rows(x_ref[...].astype(F32), q_ref, s_ref)


def _quantize_call(x, name, bm=512):
  t, d = x.shape
  return pl.pallas_call(
      _quantize_kernel,
      grid=(t // bm,),
      in_specs=[_row_spec(bm, d)],
      out_specs=[_row_spec(bm, d), _row_spec(bm, SCALE_LANES)],
      out_shape=[jax.ShapeDtypeStruct((t, d), FP8), jax.ShapeDtypeStruct((t, SCALE_LANES), F32)],
      compiler_params=_params(("arbitrary",)),
      name=name,
  )(x)


def _matmul_fp8(x8, x_scale, w8, w_scale, out_dtype, name, bm=1024, bn=1024):
  m, k = x8.shape
  n = w8.shape[1]
  bm, bn = min(bm, m), min(bn, n)
  return pl.pallas_call(
      _mm8_kernel,
      grid=(m // bm, n // bn),
      in_specs=[pl.BlockSpec((bm, k), lambda i, j: (i, 0)),
                pl.BlockSpec((bm, x_scale.shape[1]), lambda i, j: (i, 0)),
                pl.BlockSpec((k, bn), lambda i, j: (0, j)),
                pl.BlockSpec((1, bn), lambda i, j: (0, j))],
      out_specs=pl.BlockSpec((bm, bn), lambda i, j: (i, j)),
      out_shape=jax.ShapeDtypeStruct((m, n), out_dtype),
      compiler_params=_params(("arbitrary", "arbitrary")),
      name=name,
  )(x8, x_scale, w8, w_scale)


def _matmul(x, w, out_dtype, name, bm=1024, bn=1024):
  m, k = x.shape
  n = w.shape[1]
  bm, bn = min(bm, m), min(bn, n)
  return pl.pallas_call(
      _mm_kernel,
      grid=(m // bm, n // bn),
      in_specs=[pl.BlockSpec((bm, k), lambda i, j: (i, 0)),
                pl.BlockSpec((k, bn), lambda i, j: (0, j))],
      out_specs=pl.BlockSpec((bm, bn), lambda i, j: (i, j)),
      out_shape=jax.ShapeDtypeStruct((m, n), out_dtype),
      compiler_params=_params(("arbitrary", "arbitrary")),
      name=name,
  )(x, w)


def _causal_conv(hist_ref, raw, w_ref, b_ref, cols=slice(None)):
  rows, c = raw.shape
  x3 = raw.astype(F32).reshape(rows // HIST_ROWS, HIST_ROWS, c)
  ext = jnp.concatenate([hist_ref[:, cols][None], x3], axis=0)
  hist_ref[:, cols] = x3[rows // HIST_ROWS - 1]
  sub = lax.broadcasted_iota(jnp.int32, x3.shape, 1)
  y = b_ref[:, cols] + w_ref[CONV_TAPS - 1:CONV_TAPS, cols] * x3
  for s in range(1, CONV_TAPS):
    rot = pltpu.roll(ext, s, axis=1)
    shifted = jnp.where(sub >= s, rot[1:], rot[:-1])
    y = y + w_ref[CONV_TAPS - 1 - s:CONV_TAPS - s, cols] * shifted
  return y


def _softplus(x):
  return jnp.maximum(x, 0.0) + jnp.log1p(jnp.exp(-jnp.abs(x)))


def _ssd_kernel(z_ref, x_ref, bc_ref, dt_ref, cwx_ref, cbx_ref, cwbc_ref, cbbc_ref,
                dtb_ref, alog_ref, dsk_ref, nw_ref, o_ref, xh_ref, bch_ref, st_ref):
  L = x_ref.shape[0]
  n = SSD_STATE
  gw = SSD_GROUP_WIDTH
  pair_w = 2 * SSD_HEAD_DIM
  hpg = SSD_HEADS // SSD_GROUPS

  @pl.when(pl.program_id(1) == 0)
  def _init():
    xh_ref[...] = jnp.zeros(xh_ref.shape, F32)
    bch_ref[...] = jnp.zeros(bch_ref.shape, F32)
    st_ref[...] = jnp.zeros(st_ref.shape, F32)

  dt = _softplus(dt_ref[...] + dtb_ref[...])
  acs = dt * (-jnp.exp(alog_ref[...]))
  row_t = lax.broadcasted_iota(jnp.int32, acs.shape, 0)
  shift = 1
  while shift < L:
    acs = acs + jnp.where(row_t >= shift, pltpu.roll(acs, shift, axis=0), 0.0)
    shift *= 2
  dt_t = dt.T
  acs_t = acs.T
  a_last = acs_t[:, L - 1:L]
  w_t = dt_t * jnp.exp(a_last - acs_t)
  dfs_all = jnp.exp(acs)
  cd_all = jnp.exp(a_last)

  row = lax.broadcasted_iota(jnp.int32, (L, L), 0)
  col = lax.broadcasted_iota(jnp.int32, (L, L), 1)
  causal = row >= col
  lo = lax.broadcasted_iota(jnp.int32, (L, pair_w), 1) < SSD_HEAD_DIM
  lo_n = lax.broadcasted_iota(jnp.int32, (n, pair_w), 1) < SSD_HEAD_DIM

  def conv_silu(hist_ref, raw_ref, w_ref, bias_ref, cols):
    y = _causal_conv(hist_ref, raw_ref[:, cols], w_ref, bias_ref, cols)
    y = y * jax.nn.sigmoid(y)
    return y.reshape(L, y.shape[2])

  for g in range(SSD_GROUPS):
    gcols = slice(g * gw, (g + 1) * gw)
    xs = conv_silu(xh_ref, x_ref, cwx_ref, cbx_ref, gcols)
    bm = conv_silu(bch_ref, bc_ref, cwbc_ref, cbbc_ref, slice(g * n, (g + 1) * n))
    cm = conv_silu(bch_ref, bc_ref, cwbc_ref, cbbc_ref,
                   slice((SSD_GROUPS + g) * n, (SSD_GROUPS + g + 1) * n))
    bm16 = bm.astype(BF16)
    cm16 = cm.astype(BF16)
    cb = _dot_nt(cm16, bm16)
    b_t = bm.T
    st = st_ref[g]
    y_off = _dot(cm16, st.astype(BF16))

    ys = []
    for q in range(gw // pair_w):
      sl = slice(q * pair_w, (q + 1) * pair_w)
      h0 = g * hpg + 2 * q
      xq = xs[:, sl]
      rhs = jnp.concatenate([jnp.where(lo, xq, 0.0), jnp.where(lo, 0.0, xq)],
                            axis=0).astype(BF16)
      m_parts, bw_parts = [], []
      for h in (h0, h0 + 1):
        seg = acs[:, h:h + 1] - acs_t[h:h + 1, :]
        lm = jnp.exp(jnp.where(causal, seg, -jnp.inf))
        m_parts.append((cb * lm * dt_t[h:h + 1, :]).astype(BF16))
        bw_parts.append((b_t * w_t[h:h + 1, :]).astype(BF16))
      lhs = jnp.concatenate([jnp.concatenate(m_parts, axis=1),
                             jnp.concatenate(bw_parts, axis=1)], axis=0)
      res = _dot(lhs, rhs)
      dfs = jnp.where(lo, dfs_all[:, h0:h0 + 1], dfs_all[:, h0 + 1:h0 + 2])
      ys.append(res[:L] + y_off[:, sl] * dfs)
      chunk_decay = jnp.where(lo_n, cd_all[h0:h0 + 1, :], cd_all[h0 + 1:h0 + 2, :])
      st_ref[g, :, sl] = st[:, sl] * chunk_decay + res[L:]

    y = jnp.concatenate(ys, axis=1) + dsk_ref[:, gcols] * xs
    zf = z_ref[:, gcols].astype(F32)
    y = y * (zf * jax.nn.sigmoid(zf))
    ms = jnp.mean(y * y, axis=-1, keepdims=True)
    o_ref[:, gcols] = (y * lax.rsqrt(ms + RMS_EPS) * nw_ref[:, gcols]).astype(BF16)


def _ssd_call(proj, dt_raw, conv_w, conv_b, dt_bias, a_log, d_skip, norm_w, batch, seq):
  t = proj.shape[0]
  L, n, d = SSD_CHUNK, SSD_STATE, D_MODEL
  nc = seq // L
  lanes = dt_raw.shape[1]
  pad = lanes - SSD_HEADS
  d_cols = jnp.repeat(d_skip, SSD_HEAD_DIM).reshape(1, d)

  def rows(b, c):
    return b * nc + c

  full = lambda r, c: pl.BlockSpec((r, c), lambda b, s: (0, 0))
  in_specs = [
      pl.BlockSpec((L, d), lambda b, c: (rows(b, c), COL_Z // d)),
      pl.BlockSpec((L, d), lambda b, c: (rows(b, c), COL_XS // d)),
      pl.BlockSpec((L, BC_WIDTH), lambda b, c: (rows(b, c), COL_BC // BC_WIDTH)),
      pl.BlockSpec((L, lanes), lambda b, c: (rows(b, c), 0)),
      full(CONV_TAPS, d), full(1, d), full(CONV_TAPS, BC_WIDTH), full(1, BC_WIDTH),
      full(1, lanes), full(1, lanes), full(1, d), full(1, d),
  ]
  return pl.pallas_call(
      _ssd_kernel,
      grid=(batch, nc),
      in_specs=in_specs,
      out_specs=pl.BlockSpec((L, d), lambda b, c: (rows(b, c), 0)),
      out_shape=jax.ShapeDtypeStruct((t, d), BF16),
      scratch_shapes=[
          pltpu.VMEM((HIST_ROWS, d), F32),
          pltpu.VMEM((HIST_ROWS, BC_WIDTH), F32),
          pltpu.VMEM((SSD_GROUPS, n, SSD_GROUP_WIDTH), F32),
      ],
      compiler_params=_params(("arbitrary", "arbitrary")),
      name="ssd_scan",
  )(proj, proj, proj, dt_raw, conv_w[:, :d], conv_b[:d].reshape(1, d), conv_w[:, d:],
    conv_b[d:].reshape(1, BC_WIDTH), jnp.pad(dt_bias, (0, pad)).reshape(1, lanes),
    jnp.pad(a_log, (0, pad)).reshape(1, lanes), d_cols, norm_w.reshape(1, d))


def _lru_kernel(x_ref, g_ref, cw_ref, cb_ref, wa_ref, ba_ref, wi_ref, bi_ref, lam_ref,
                o_ref, xh_ref, h_ref):
  ts, bd = x_ref.shape
  nt = ts // HIST_ROWS

  @pl.when(pl.program_id(2) == 0)
  def _init():
    xh_ref[...] = jnp.zeros(xh_ref.shape, F32)
    h_ref[...] = jnp.zeros(h_ref.shape, F32)

  xr3 = _causal_conv(xh_ref, x_ref[...], cw_ref, cb_ref)
  xr = xr3.reshape(ts, bd)
  xr16 = xr.astype(BF16)
  r = jax.nn.sigmoid(_dot(xr16, wa_ref[...]) + ba_ref[...])
  i = jax.nn.sigmoid(_dot(xr16, wi_ref[...]) + bi_ref[...])
  log_a = (-LRU_C) * r * _softplus(-lam_ref[...])
  a = jnp.exp(log_a)
  th = jnp.tanh(log_a)
  u = jnp.exp(0.5 * jnp.log(-2.0 * th / (1.0 - th))) * (i * xr)

  a = a.reshape(nt, HIST_ROWS, bd)
  u = u.reshape(nt, HIST_ROWS, bd)
  sub = lax.broadcasted_iota(jnp.int32, a.shape, 1)
  shift = 1
  while shift < HIST_ROWS:
    keep = sub >= shift
    a_prev = jnp.where(keep, pltpu.roll(a, shift, axis=1), 1.0)
    u_prev = jnp.where(keep, pltpu.roll(u, shift, axis=1), 0.0)
    u = a * u_prev + u
    a = a * a_prev
    shift *= 2
  carry = h_ref[...]
  hs = []
  for k in range(nt):
    hk = a[k] * carry + u[k]
    hs.append(hk)
    carry = jnp.broadcast_to(hk[HIST_ROWS - 1:HIST_ROWS, :], hk.shape)
  h_ref[...] = carry
  h = jnp.concatenate(hs, axis=0)
  o_ref[...] = (jax.nn.gelu(g_ref[...].astype(F32)) * h).astype(BF16)


def _lru_call(proj, conv_w, conv_b, w_a, b_a, w_i, b_i, lam, batch, seq, ts=512):
  t = proj.shape[0]
  bd = LRU_BLOCK_DIM
  nt = seq // ts

  def rows(b, k, s):
    return b * nt + s

  vec = lambda: pl.BlockSpec((1, bd), lambda b, k, s: (0, k))
  in_specs = [
      pl.BlockSpec((ts, bd), lambda b, k, s: (rows(b, k, s), COL_LRU_X // bd + k)),
      pl.BlockSpec((ts, bd), lambda b, k, s: (rows(b, k, s), COL_LRU_GATE // bd + k)),
      pl.BlockSpec((CONV_TAPS, bd), lambda b, k, s: (0, k)),
      vec(),
      pl.BlockSpec((None, bd, bd), lambda b, k, s: (k, 0, 0)),
      vec(),
      pl.BlockSpec((None, bd, bd), lambda b, k, s: (k, 0, 0)),
      vec(),
      vec(),
  ]
  return pl.pallas_call(
      _lru_kernel,
      grid=(batch, LRU_BLOCKS, nt),
      in_specs=in_specs,
      out_specs=pl.BlockSpec((ts, bd), lambda b, k, s: (rows(b, k, s), k)),
      out_shape=jax.ShapeDtypeStruct((t, D_MODEL), BF16),
      scratch_shapes=[pltpu.VMEM((HIST_ROWS, bd), F32), pltpu.VMEM((HIST_ROWS, bd), F32)],
      compiler_params=_params(("arbitrary", "arbitrary", "arbitrary")),
      name="rglru_scan",
  )(proj, proj, conv_w, conv_b.reshape(1, -1), w_a.astype(BF16), b_a.reshape(1, -1),
    w_i.astype(BF16), b_i.reshape(1, -1), lam.reshape(1, -1))


def _merge_kernel(ys_ref, yss_ref, yl_ref, yls_ref, ws_ref, wss_ref, wl_ref, wls_ref,
                  ga_ref, gb_ref, o_ref):
  bn = o_ref.shape[1]
  rep = bn // yss_ref.shape[1]
  pa = _dot(ys_ref[...], ws_ref[...]) * pltpu.repeat(yss_ref[...], rep, axis=1) * wss_ref[...]
  pb = _dot(yl_ref[...], wl_ref[...]) * pltpu.repeat(yls_ref[...], rep, axis=1) * wls_ref[...]
  ga = jax.nn.sigmoid(ga_ref[...].astype(F32))
  gb = jax.nn.sigmoid(gb_ref[...].astype(F32))
  o_ref[...] = (ga * pa + gb * pb).astype(o_ref.dtype)


def _merge_call(ys8, ys_scale, yl8, yl_scale, ws8, ws_scale, wl8, wl_scale, proj, bm=1024, bn=512):
  t, d = ys8.shape
  row = lambda w: pl.BlockSpec((bm, w), lambda i, j: (i, 0))
  col = lambda r: pl.BlockSpec((r, bn), lambda i, j: (0, j))
  return pl.pallas_call(
      _merge_kernel,
      grid=(t // bm, d // bn),
      in_specs=[
          row(d), row(SCALE_LANES), row(d), row(SCALE_LANES),
          col(d), col(1), col(d), col(1),
          pl.BlockSpec((bm, bn), lambda i, j: (i, COL_GATE_SSD // bn + j)),
          pl.BlockSpec((bm, bn), lambda i, j: (i, COL_GATE_LRU // bn + j)),
      ],
      out_specs=pl.BlockSpec((bm, bn), lambda i, j: (i, j)),
      out_shape=jax.ShapeDtypeStruct((t, d), BF16),
      compiler_params=_params(("arbitrary", "arbitrary")),
      name="gated_merge",
  )(ys8, ys_scale, yl8, yl_scale, ws8, ws_scale, wl8, wl_scale, proj, proj)


def _xattn_kernel(h0_ref, mix_ref, g1_ref, b1_ref, wq_ref, k_ref, v_ref, wo_ref, g_ref, b_ref,
                  o_ref, ob_ref, obt_ref, osc_ref):
  hd = XATTN_HEAD_DIM
  h = _layer_norm(DEEPNORM_ALPHA * h0_ref[...] + mix_ref[...], g1_ref[...], b1_ref[...])
  q = _dot(h.astype(BF16), wq_ref[...]).astype(BF16)
  k = k_ref[...]
  v = v_ref[...]
  outs = []
  for head in range(XATTN_HEADS):
    sl = slice(head * hd, (head + 1) * hd)
    s = _dot_nt(q[:, sl], k[:, sl]) * (hd ** -0.5)
    s = s - jnp.max(s, axis=-1, keepdims=True)
    p = jnp.exp(s)
    p = p / jnp.sum(p, axis=-1, keepdims=True)
    outs.append(_dot(p.astype(BF16), v[:, sl]))
  o = jnp.concatenate(outs, axis=1).astype(BF16)
  y = _dot(o, wo_ref[...])
  out = _layer_norm(DEEPNORM_ALPHA * h + y, g_ref[...], b_ref[...])
  o_ref[...] = out
  ob_ref[...] = out.astype(BF16)
  out_t = out.T
  amax = jnp.maximum(jnp.max(jnp.abs(out_t), axis=0, keepdims=True), FP8_AMAX_FLOOR)
  obt_ref[...] = (out_t * (FP8_TARGET / amax)).astype(FP8)
  osc_ref[0:1, :] = amax * (1.0 / FP8_TARGET)
  osc_ref[1:2, :] = jnp.sqrt(jnp.sum(out_t * out_t, axis=0, keepdims=True))


def _xattn_call(h0, mixed, g1, b1, w_q, k, v, w_o, g, b, seq, tm=256):
  t, d = h0.shape
  xw = w_q.shape[1]
  mem_len = k.shape[1]
  per_batch = seq // tm
  return pl.pallas_call(
      _xattn_kernel,
      grid=(t // tm,),
      in_specs=[
          _row_spec(tm, d), _row_spec(tm, d), _vec_spec(d), _vec_spec(d),
          pl.BlockSpec((d, xw), lambda i: (0, 0)),
          pl.BlockSpec((None, mem_len, xw), lambda i: (i // per_batch, 0, 0)),
          pl.BlockSpec((None, mem_len, xw), lambda i: (i // per_batch, 0, 0)),
          pl.BlockSpec((xw, d), lambda i: (0, 0)),
          _vec_spec(d), _vec_spec(d),
      ],
      out_specs=[_row_spec(tm, d), _row_spec(tm, d), pl.BlockSpec((d, tm), lambda i: (0, i)),
                 pl.BlockSpec((2, tm), lambda i: (0, i))],
      out_shape=[jax.ShapeDtypeStruct((t, d), F32), jax.ShapeDtypeStruct((t, d), BF16),
                 jax.ShapeDtypeStruct((d, t), FP8), jax.ShapeDtypeStruct((2, t), F32)],
      compiler_params=_params(("arbitrary",)),
      name="ln1_mem_xattn_ln2",
  )(h0, mixed, g1.reshape(1, d), b1.reshape(1, d), w_q, k, v, w_o, g.reshape(1, d),
    b.reshape(1, d))


_PEER_CAND_PAIRS = tuple((a, b) for a in range(PEER_TOPK) for b in range(PEER_TOPK)
                         if (a + 1) * (b + 1) <= PEER_TOPK)
_PEER_CAND_ROWS = 56


def _peer_route_kernel(q_ref, k1_ref, k2_ref, s1_ref, s2_ref, st_ref, cand_ref):
  tm = q_ref.shape[0]
  kd = PEER_N_KEYS
  neg_inf = -jnp.inf

  def top_values(s):
    vals = []
    cur = s
    for _ in range(PEER_TOPK):
      m = jnp.max(cur, axis=0, keepdims=True)
      vals.append(m)
      cur = jnp.where(cur == m, neg_inf, cur)
    return vals

  cand_ref[...] = jnp.full(cand_ref.shape, neg_inf, F32)
  for head in range(PEER_HEADS):
    q1 = q_ref[:, head * 2 * kd:head * 2 * kd + kd]
    q2 = q_ref[:, head * 2 * kd + kd:(head + 1) * 2 * kd]
    s1 = _dot_nt(k1_ref[head], q1)
    s2 = _dot_nt(k2_ref[head], q2)
    s1_ref[head] = s1
    s2_ref[head] = s2
    v1 = top_values(s1)
    v2 = top_values(s2)
    for idx, (a, b) in enumerate(_PEER_CAND_PAIRS):
      cand_ref[idx:idx + 1, :] = v1[a] + v2[b]
    cand = cand_ref[...]
    cur = cand
    count = jnp.zeros((1, tm), F32)
    tau = jnp.full((1, tm), neg_inf, F32)
    for _ in range(PEER_TOPK):
      m = jnp.max(cur, axis=0, keepdims=True)
      eq = cur == m
      new_count = count + jnp.sum(jnp.where(eq, 1.0, 0.0), axis=0, keepdims=True)
      tau = jnp.where(count < PEER_TOPK, jnp.where(new_count >= PEER_TOPK, m, tau), tau)
      count = new_count
      cur = jnp.where(eq, neg_inf, cur)
    top = v1[0] + v2[0]
    z = jnp.sum(jnp.where(cand >= tau, jnp.exp(cand - top), 0.0), axis=0, keepdims=True)
    st_ref[0, head:head + 1, :] = tau
    st_ref[1, head:head + 1, :] = v1[0]
    st_ref[2, head:head + 1, :] = v2[0]
    st_ref[3, head:head + 1, :] = 1.0 / z


def _peer_route_call(q, keys_1, keys_2, tm=256):
  t = q.shape[0]
  h, kd = PEER_HEADS, PEER_N_KEYS
  score_spec = pl.BlockSpec((h, kd, tm), lambda i: (0, 0, i))
  return pl.pallas_call(
      _peer_route_kernel,
      grid=(t // tm,),
      in_specs=[pl.BlockSpec((tm, q.shape[1]), lambda i: (i, 0)),
                pl.BlockSpec((h, kd, kd), lambda i: (0, 0, 0)),
                pl.BlockSpec((h, kd, kd), lambda i: (0, 0, 0))],
      out_specs=[score_spec, score_spec, pl.BlockSpec((4, h, tm), lambda i: (0, 0, i))],
      out_shape=[jax.ShapeDtypeStruct((h, kd, t), F32), jax.ShapeDtypeStruct((h, kd, t), F32),
                 jax.ShapeDtypeStruct((4, h, t), F32)],
      scratch_shapes=[pltpu.VMEM((_PEER_CAND_ROWS, tm), F32)],
      compiler_params=_params(("arbitrary",)),
      name="peer_route",
  )(q, keys_1, keys_2)


BF16_ROWS = 16


def _peer_dense_kernel(xt_ref, tsc_ref, u_ref, v_ref, s1_ref, s2_ref, st_ref, o_ref,
                       d2_ref, e2_ref):
  j = pl.program_id(1)
  tm = xt_ref.shape[1]
  half = u_ref.shape[0] // 2
  kd = PEER_N_KEYS
  rows_per_half = half // kd
  tiles = kd // BF16_ROWS

  def packed(x):
    return x.astype(BF16).reshape(tiles, BF16_ROWS, tm)

  def packed_row(row):
    return jnp.broadcast_to(row, (BF16_ROWS, tm)).astype(BF16)[None]

  @pl.when(j == 0)
  def _init():
    o_ref[...] = jnp.zeros(o_ref.shape, F32)
    for head in range(PEER_HEADS):
      d2_ref[head] = packed(s2_ref[head] - st_ref[0, head:head + 1, :])
      e2_ref[head] = packed(jnp.exp(s2_ref[head] - st_ref[2, head:head + 1, :]))

  def gate_term(i1, head):
    s1_row = s1_ref[head, pl.ds(i1, 1), :]
    e1_row = (jnp.exp(s1_row - st_ref[1, head:head + 1, :]) * st_ref[3, head:head + 1, :]
              * tsc_ref[1:2, :])
    sel = d2_ref[head] >= packed_row(-s1_row)
    return jnp.where(sel, e2_ref[head] * packed_row(e1_row), jnp.zeros((), BF16))

  def half_tile(half_index):
    rows = slice(half_index * half, (half_index + 1) * half)
    act = _dot(u_ref[rows, :], xt_ref[...]) * tsc_ref[0:1, :]
    gates = []
    for r in range(rows_per_half):
      i1 = (2 * j + half_index) * rows_per_half + r
      g = gate_term(i1, 0)
      for head in range(1, PEER_HEADS):
        g = g + gate_term(i1, head)
      gates.append(g.reshape(kd, tm))
    p = jnp.concatenate(gates, axis=0) * jax.nn.gelu(act.astype(BF16))
    o_ref[...] += _dot(p.astype(F32).T.astype(FP8), v_ref[rows, :])

  half_tile(0)
  half_tile(1)


def _peer_dense_call(xt, tsc, u, v, s1, s2, stats, tm=512, te=512):
  d, t = xt.shape
  e = u.shape[0]
  h, kd = PEER_HEADS, PEER_N_KEYS
  score_spec = pl.BlockSpec((h, kd, tm), lambda i, j: (0, 0, i))
  packed = pltpu.VMEM((h, kd // BF16_ROWS, BF16_ROWS, tm), BF16)
  return pl.pallas_call(
      _peer_dense_kernel,
      grid=(t // tm, e // te),
      in_specs=[pl.BlockSpec((d, tm), lambda i, j: (0, i)),
                pl.BlockSpec((2, tm), lambda i, j: (0, i)),
                pl.BlockSpec((te, d), lambda i, j: (j, 0)),
                pl.BlockSpec((te, d), lambda i, j: (j, 0)),
                score_spec, score_spec,
                pl.BlockSpec((4, h, tm), lambda i, j: (0, 0, i))],
      out_specs=pl.BlockSpec((tm, d), lambda i, j: (i, 0)),
      out_shape=jax.ShapeDtypeStruct((t, d), F32),
      scratch_shapes=[packed, packed],
      compiler_params=_params(("arbitrary", "arbitrary")),
      name="peer_dense",
  )(xt, tsc, u, v, s1, s2, stats)


def kernel(x, mem, ln_in_g, ln_in_b, w_in, ssd_conv_w, ssd_conv_b, ssd_dt_bias, ssd_a_log, ssd_d, ssd_norm_w, lru_conv_w, lru_conv_b, lru_w_a, lru_b_a, lru_w_i, lru_b_i, lru_lambda, w_proj_ssd, w_proj_lru, w_mix_out, ln1_g, ln1_b, xa_w_q, xa_w_k, xa_w_v, xa_w_o, ln2_g, ln2_b, peer_w_q, peer_keys_1, peer_keys_2, peer_u, peer_v, ln3_g, ln3_b):
  batch, seq, d = x.shape
  t = batch * seq
  assert w_in.shape[0] == 1, "single-layer trunk"

  w = w_in[0]
  c_dt = SSD_SLAB_COLS
  c_lru = c_dt + SSD_HEADS
  w_dt = jnp.pad(w[:, c_dt:c_lru], ((0, 0), (0, 128 - SSD_HEADS))).astype(BF16)

  w_amax = _col_amax(w)
  w8, w_scale = _quantize_cols(w, w_amax)
  w_ssd8, w_ssd_scale = w8[:, :c_dt], w_scale[:, :c_dt]
  w_lru8, w_lru_scale = w8[:, c_lru:], w_scale[:, c_lru:]

  h0, h0b, h0q, h0_scale = _ln_call(x.reshape(t, d), ln_in_g, ln_in_b)
  proj_ssd = _matmul_fp8(h0q, h0_scale, w_ssd8, w_ssd_scale, BF16, "in_proj_ssd")
  proj_lru = _matmul_fp8(h0q, h0_scale, w_lru8, w_lru_scale, BF16, "in_proj_lru")
  dt_raw = _matmul(h0b, w_dt, F32, "dt_proj")

  y_ssd = _ssd_call(proj_ssd, dt_raw, ssd_conv_w[0], ssd_conv_b[0], ssd_dt_bias[0], ssd_a_log[0],
                    ssd_d[0], ssd_norm_w[0], batch, seq)
  y_lru = _lru_call(proj_lru, lru_conv_w[0], lru_conv_b[0], lru_w_a[0], lru_b_a[0], lru_w_i[0],
                    lru_b_i[0], lru_lambda[0], batch, seq)
  ys8, ys_scale = _quantize_call(y_ssd, "quant_y_ssd")
  yl8, yl_scale = _quantize_call(y_lru, "quant_y_lru")
  wps8, wps_scale = _quantize_cols(w_proj_ssd[0], _col_amax(w_proj_ssd[0]))
  wpl8, wpl_scale = _quantize_cols(w_proj_lru[0], _col_amax(w_proj_lru[0]))
  merged = _merge_call(ys8, ys_scale, yl8, yl_scale, wps8, wps_scale, wpl8, wpl_scale, proj_lru)
  mg8, mg_scale = _quantize_call(merged, "quant_merged")
  wmx8, wmx_scale = _quantize_cols(w_mix_out[0], _col_amax(w_mix_out[0]))
  mixed = _matmul_fp8(mg8, mg_scale, wmx8, wmx_scale, F32, "mix_out")

  mem_len = mem.shape[1]
  w_kv = jnp.concatenate([xa_w_k[0], xa_w_v[0]], axis=1).astype(BF16)
  kv = _matmul(mem.reshape(batch * mem_len, d).astype(BF16), w_kv, BF16, "mem_kv")
  xw = xa_w_k.shape[-1]
  k = kv[:, :xw].reshape(batch, mem_len, xw)
  v = kv[:, xw:].reshape(batch, mem_len, xw)
  h2, h2b, h2t8, h2t_stats = _xattn_call(h0, mixed, ln1_g[0], ln1_b[0], xa_w_q[0].astype(BF16),
                                         k, v, xa_w_o[0].astype(BF16), ln2_g[0], ln2_b[0], seq)

  q = _matmul(h2b, peer_w_q[0].astype(BF16), BF16, "peer_query")
  s1, s2, stats = _peer_route_call(q, peer_keys_1[0].astype(BF16), peer_keys_2[0].astype(BF16))
  u, v_tab = peer_u[0], peer_v[0]
  u_amax = jnp.maximum(jnp.max(jnp.abs(u)), FP8_AMAX_FLOOR)
  v_amax = jnp.maximum(jnp.max(jnp.abs(v_tab)), FP8_AMAX_FLOOR)
  u8 = (u * (FP8_TARGET / u_amax)).astype(FP8)
  v8 = (v_tab * (FP8_TARGET / v_amax)).astype(FP8)
  u_norm_max = jnp.sqrt(jnp.max(jnp.sum(u * u, axis=1)))
  p_bound = (FP8_ROUNDING_SLACK * u_norm_max
             * jnp.sum(stats[3], axis=0, keepdims=True) * h2t_stats[1:2, :])
  p_bound = jnp.maximum(p_bound, FP8_AMAX_FLOOR)
  tok_scales = jnp.concatenate([h2t_stats[0:1, :] * (u_amax * (1.0 / FP8_TARGET)),
                                FP8_TARGET / p_bound], axis=0)
  ffn_scale = jnp.broadcast_to(
      (p_bound * (v_amax * (1.0 / FP8_TARGET) ** 2)).reshape(t, 1), (t, SCALE_LANES))
  ffn = _peer_dense_call(h2t8, tok_scales, u8, v8, s1, s2, stats)
  h3, _ = _res_ln_call(h2, ffn, ln3_g[0], ln3_b[0], "res_ln3", y_scale=ffn_scale)
  return h3.reshape(batch, seq, d)
```

```python
import functools

import jax
import jax.numpy as jnp
from jax import lax
from jax.experimental import pallas as pl
from jax.experimental.pallas import tpu as pltpu

F32 = jnp.float32
BF16 = jnp.bfloat16
FP8 = jnp.float8_e4m3fn
FP8_TARGET = 384.0
FP8_AMAX_FLOOR = 1e-30
FP8_ROUNDING_SLACK = 1.25

D_MODEL = 4096
SSD_HEADS = 64
SSD_HEAD_DIM = 64
SSD_GROUPS = 8
SSD_STATE = 128
SSD_CHUNK = 128
SSD_GROUP_WIDTH = D_MODEL // SSD_GROUPS
LRU_BLOCKS = 16
LRU_BLOCK_DIM = 256
LRU_C = 8.0
XATTN_HEADS = 4
XATTN_HEAD_DIM = 128
PEER_HEADS = 8
PEER_N_KEYS = 128
PEER_TOPK = 16
DEEPNORM_ALPHA = 2.0 ** 0.25
LN_EPS = 1e-5
RMS_EPS = 1e-6
CONV_TAPS = 4
HIST_ROWS = 8

VMEM_LIMIT_BYTES = 58 * 1024 * 1024

COL_Z, COL_XS, COL_BC = 0, 4096, 8192
SSD_SLAB_COLS = 10240
COL_LRU_GATE, COL_LRU_X, COL_GATE_SSD, COL_GATE_LRU = 0, 4096, 8192, 12288
BC_WIDTH = 2 * SSD_GROUPS * SSD_STATE


def _params(sem, flags=None):
  return pltpu.CompilerParams(dimension_semantics=sem, vmem_limit_bytes=VMEM_LIMIT_BYTES,
                              flags=flags)


def _layer_norm(x, g, b):
  mu = jnp.mean(x, axis=-1, keepdims=True)
  xc = x - mu
  var = jnp.mean(xc * xc, axis=-1, keepdims=True)
  return xc * lax.rsqrt(var + LN_EPS) * g + b


def _dot(a, b):
  return jnp.dot(a, b, preferred_element_type=F32)


def _dot_nt(a, b):
  return lax.dot_general(a, b, (((1,), (1,)), ((), ())), preferred_element_type=F32)


SCALE_LANES = 128


def _quantize_rows(y, q_ref, s_ref):
  amax = jnp.maximum(jnp.max(jnp.abs(y), axis=-1, keepdims=True), FP8_AMAX_FLOOR)
  q_ref[...] = (y * (FP8_TARGET / amax)).astype(FP8)
  s_ref[...] = jnp.broadcast_to(amax * (1.0 / FP8_TARGET), s_ref.shape)


def _ln_kernel(x_ref, g_ref, b_ref, o_ref, ob_ref, oq_ref, os_ref):
  y = _layer_norm(x_ref[...], g_ref[...], b_ref[...])
  o_ref[...] = y
  ob_ref[...] = y.astype(BF16)
  _quantize_rows(y, oq_ref, os_ref)


def _res_ln_kernel(h_ref, y_ref, g_ref, b_ref, o_ref, ob_ref):
  y = _layer_norm(DEEPNORM_ALPHA * h_ref[...] + y_ref[...], g_ref[...], b_ref[...])
  o_ref[...] = y
  ob_ref[...] = y.astype(BF16)


def _res_ln_scaled_kernel(h_ref, y_ref, s_ref, g_ref, b_ref, o_ref, ob_ref):
  y = y_ref[...] * pltpu.repeat(s_ref[...], y_ref.shape[1] // s_ref.shape[1], axis=1)
  out = _layer_norm(DEEPNORM_ALPHA * h_ref[...] + y, g_ref[...], b_ref[...])
  o_ref[...] = out
  ob_ref[...] = out.astype(BF16)


def _row_spec(bm, d):
  return pl.BlockSpec((bm, d), lambda i: (i, 0))


def _vec_spec(d):
  return pl.BlockSpec((1, d), lambda i: (0, 0))


def _ln_call(x, g, b, bm=256):
  t, d = x.shape
  return pl.pallas_call(
      _ln_kernel,
      grid=(t // bm,),
      in_specs=[_row_spec(bm, d), _vec_spec(d), _vec_spec(d)],
      out_specs=[_row_spec(bm, d), _row_spec(bm, d), _row_spec(bm, d),
                 _row_spec(bm, SCALE_LANES)],
      out_shape=[jax.ShapeDtypeStruct((t, d), F32), jax.ShapeDtypeStruct((t, d), BF16),
                 jax.ShapeDtypeStruct((t, d), FP8), jax.ShapeDtypeStruct((t, SCALE_LANES), F32)],
      compiler_params=_params(("arbitrary",)),
      name="ln_in",
  )(x, g.reshape(1, d), b.reshape(1, d))


def _res_ln_call(h, y, g, b, name, bm=256, y_scale=None):
  t, d = h.shape
  kernel_fn, extra, extra_specs = _res_ln_kernel, (), []
  if y_scale is not None:
    kernel_fn, extra, extra_specs = _res_ln_scaled_kernel, (y_scale,), [_row_spec(bm, SCALE_LANES)]
  return pl.pallas_call(
      kernel_fn,
      grid=(t // bm,),
      in_specs=[_row_spec(bm, d), _row_spec(bm, d)] + extra_specs + [_vec_spec(d), _vec_spec(d)],
      out_specs=[_row_spec(bm, d), _row_spec(bm, d)],
      out_shape=[jax.ShapeDtypeStruct((t, d), F32), jax.ShapeDtypeStruct((t, d), BF16)],
      compiler_params=_params(("arbitrary",)),
      name=name,
  )(h, y, *extra, g.reshape(1, d), b.reshape(1, d))


def _mm_kernel(x_ref, w_ref, o_ref):
  o_ref[...] = _dot(x_ref[...], w_ref[...]).astype(o_ref.dtype)


def _mm8_kernel(x_ref, xs_ref, w_ref, ws_ref, o_ref):
  acc = _dot(x_ref[...], w_ref[...])
  xs = pltpu.repeat(xs_ref[...], acc.shape[1] // xs_ref.shape[1], axis=1)
  o_ref[...] = (acc * xs * ws_ref[...]).astype(o_ref.dtype)


def _col_amax_kernel(w_ref, o_ref):
  @pl.when(pl.program_id(0) == 0)
  def _init():
    o_ref[...] = jnp.zeros(o_ref.shape, F32)

  bk, n = w_ref.shape
  x = jnp.abs(w_ref[...]).reshape(bk // HIST_ROWS, HIST_ROWS, n)
  o_ref[...] = jnp.maximum(o_ref[...], jnp.max(x, axis=0))


def _col_amax(w):
  k, n = w.shape
  bk = 512 if n <= D_MODEL else 64
  part = pl.pallas_call(
      _col_amax_kernel,
      grid=(k // bk,),
      in_specs=[pl.BlockSpec((bk, n), lambda i: (i, 0))],
      out_specs=pl.BlockSpec((HIST_ROWS, n), lambda i: (0, 0)),
      out_shape=jax.ShapeDtypeStruct((HIST_ROWS, n), F32),
      compiler_params=_params(("arbitrary",)),
      name="col_amax",
  )(w)
  return jnp.maximum(jnp.max(part, axis=0, keepdims=True), FP8_AMAX_FLOOR)


def _quantize_cols(w, w_amax):
  return (w * (FP8_TARGET / w_amax)).astype(FP8), w_amax * (1.0 / FP8_TARGET)


def _quantize_kernel(x_ref, q_ref, s_ref):
  _quantize_rows(x_ref[...].astype(F32), q_ref, s_ref)


def _quantize_call(x, name, bm=512):
  t, d = x.shape
  return pl.pallas_call(
      _quantize_kernel,
      grid=(t // bm,),
      in_specs=[_row_spec(bm, d)],
      out_specs=[_row_spec(bm, d), _row_spec(bm, SCALE_LANES)],
      out_shape=[jax.ShapeDtypeStruct((t, d), FP8), jax.ShapeDtypeStruct((t, SCALE_LANES), F32)],
      compiler_params=_params(("arbitrary",)),
      name=name,
  )(x)


def _matmul_fp8(x8, x_scale, w8, w_scale, out_dtype, name, bm=1024, bn=1024):
  m, k = x8.shape
  n = w8.shape[1]
  bm, bn = min(bm, m), min(bn, n)
  return pl.pallas_call(
      _mm8_kernel,
      grid=(m // bm, n // bn),
      in_specs=[pl.BlockSpec((bm, k), lambda i, j: (i, 0)),
                pl.BlockSpec((bm, x_scale.shape[1]), lambda i, j: (i, 0)),
                pl.BlockSpec((k, bn), lambda i, j: (0, j)),
                pl.BlockSpec((1, bn), lambda i, j: (0, j))],
      out_specs=pl.BlockSpec((bm, bn), lambda i, j: (i, j)),
      out_shape=jax.ShapeDtypeStruct((m, n), out_dtype),
      compiler_params=_params(("arbitrary", "arbitrary")),
      name=name,
  )(x8, x_scale, w8, w_scale)


def _matmul(x, w, out_dtype, name, bm=1024, bn=1024):
  m, k = x.shape
  n = w.shape[1]
  bm, bn = min(bm, m), min(bn, n)
  return pl.pallas_call(
      _mm_kernel,
      grid=(m // bm, n // bn),
      in_specs=[pl.BlockSpec((bm, k), lambda i, j: (i, 0)),
                pl.BlockSpec((k, bn), lambda i, j: (0, j))],
      out_specs=pl.BlockSpec((bm, bn), lambda i, j: (i, j)),
      out_shape=jax.ShapeDtypeStruct((m, n), out_dtype),
      compiler_params=_params(("arbitrary", "arbitrary")),
      name=name,
  )(x, w)


def _causal_conv(hist_ref, raw, w_ref, b_ref, cols=slice(None)):
  rows, c = raw.shape
  x3 = raw.astype(F32).reshape(rows // HIST_ROWS, HIST_ROWS, c)
  ext = jnp.concatenate([hist_ref[:, cols][None], x3], axis=0)
  hist_ref[:, cols] = x3[rows // HIST_ROWS - 1]
  sub = lax.broadcasted_iota(jnp.int32, x3.shape, 1)
  y = b_ref[:, cols] + w_ref[CONV_TAPS - 1:CONV_TAPS, cols] * x3
  for s in range(1, CONV_TAPS):
    rot = pltpu.roll(ext, s, axis=1)
    shifted = jnp.where(sub >= s, rot[1:], rot[:-1])
    y = y + w_ref[CONV_TAPS - 1 - s:CONV_TAPS - s, cols] * shifted
  return y


def _softplus(x):
  return jnp.maximum(x, 0.0) + jnp.log1p(jnp.exp(-jnp.abs(x)))


def _ssd_kernel(z_ref, x_ref, bc_ref, dt_ref, cwx_ref, cbx_ref, cwbc_ref, cbbc_ref,
                dtb_ref, alog_ref, dsk_ref, nw_ref, o_ref, xh_ref, bch_ref, st_ref):
  L = x_ref.shape[0]
  n = SSD_STATE
  gw = SSD_GROUP_WIDTH
  pair_w = 2 * SSD_HEAD_DIM
  hpg = SSD_HEADS // SSD_GROUPS

  @pl.when(pl.program_id(1) == 0)
  def _init():
    xh_ref[...] = jnp.zeros(xh_ref.shape, F32)
    bch_ref[...] = jnp.zeros(bch_ref.shape, F32)
    st_ref[...] = jnp.zeros(st_ref.shape, F32)

  dt = _softplus(dt_ref[...] + dtb_ref[...])
  acs = dt * (-jnp.exp(alog_ref[...]))
  row_t = lax.broadcasted_iota(jnp.int32, acs.shape, 0)
  shift = 1
  while shift < L:
    acs = acs + jnp.where(row_t >= shift, pltpu.roll(acs, shift, axis=0), 0.0)
    shift *= 2
  dt_t = dt.T
  acs_t = acs.T
  a_last = acs_t[:, L - 1:L]
  w_t = dt_t * jnp.exp(a_last - acs_t)
  dfs_all = jnp.exp(acs)
  cd_all = jnp.exp(a_last)

  row = lax.broadcasted_iota(jnp.int32, (L, L), 0)
  col = lax.broadcasted_iota(jnp.int32, (L, L), 1)
  causal = row >= col
  lo = lax.broadcasted_iota(jnp.int32, (L, pair_w), 1) < SSD_HEAD_DIM
  lo_n = lax.broadcasted_iota(jnp.int32, (n, pair_w), 1) < SSD_HEAD_DIM

  def conv_silu(hist_ref, raw_ref, w_ref, bias_ref, cols):
    y = _causal_conv(hist_ref, raw_ref[:, cols], w_ref, bias_ref, cols)
    y = y * jax.nn.sigmoid(y)
    return y.reshape(L, y.shape[2])

  for g in range(SSD_GROUPS):
    gcols = slice(g * gw, (g + 1) * gw)
    xs = conv_silu(xh_ref, x_ref, cwx_ref, cbx_ref, gcols)
    bm = conv_silu(bch_ref, bc_ref, cwbc_ref, cbbc_ref, slice(g * n, (g + 1) * n))
    cm = conv_silu(bch_ref, bc_ref, cwbc_ref, cbbc_ref,
                   slice((SSD_GROUPS + g) * n, (SSD_GROUPS + g + 1) * n))
    bm16 = bm.astype(BF16)
    cm16 = cm.astype(BF16)
    cb = _dot_nt(cm16, bm16)
    b_t = bm.T
    st = st_ref[g]
    y_off = _dot(cm16, st.astype(BF16))

    ys = []
    for q in range(gw // pair_w):
      sl = slice(q * pair_w, (q + 1) * pair_w)
      h0 = g * hpg + 2 * q
      xq = xs[:, sl]
      rhs = jnp.concatenate([jnp.where(lo, xq, 0.0), jnp.where(lo, 0.0, xq)],
                            axis=0).astype(BF16)
      m_parts, bw_parts = [], []
      for h in (h0, h0 + 1):
        seg = acs[:, h:h + 1] - acs_t[h:h + 1, :]
        lm = jnp.exp(jnp.where(causal, seg, -jnp.inf))
        m_parts.append((cb * lm * dt_t[h:h + 1, :]).astype(BF16))
        bw_parts.append((b_t * w_t[h:h + 1, :]).astype(BF16))
      lhs = jnp.concatenate([jnp.concatenate(m_parts, axis=1),
                             jnp.concatenate(bw_parts, axis=1)], axis=0)
      res = _dot(lhs, rhs)
      dfs = jnp.where(lo, dfs_all[:, h0:h0 + 1], dfs_all[:, h0 + 1:h0 + 2])
      ys.append(res[:L] + y_off[:, sl] * dfs)
      chunk_decay = jnp.where(lo_n, cd_all[h0:h0 + 1, :], cd_all[h0 + 1:h0 + 2, :])
      st_ref[g, :, sl] = st[:, sl] * chunk_decay + res[L:]

    y = jnp.concatenate(ys, axis=1) + dsk_ref[:, gcols] * xs
    zf = z_ref[:, gcols].astype(F32)
    y = y * (zf * jax.nn.sigmoid(zf))
    ms = jnp.mean(y * y, axis=-1, keepdims=True)
    o_ref[:, gcols] = (y * lax.rsqrt(ms + RMS_EPS) * nw_ref[:, gcols]).astype(BF16)


def _ssd_call(proj, dt_raw, conv_w, conv_b, dt_bias, a_log, d_skip, norm_w, batch, seq):
  t = proj.shape[0]
  L, n, d = SSD_CHUNK, SSD_STATE, D_MODEL
  nc = seq // L
  lanes = dt_raw.shape[1]
  pad = lanes - SSD_HEADS
  d_cols = jnp.repeat(d_skip, SSD_HEAD_DIM).reshape(1, d)

  def rows(b, c):
    return b * nc + c

  full = lambda r, c: pl.BlockSpec((r, c), lambda b, s: (0, 0))
  in_specs = [
      pl.BlockSpec((L, d), lambda b, c: (rows(b, c), COL_Z // d)),
      pl.BlockSpec((L, d), lambda b, c: (rows(b, c), COL_XS // d)),
      pl.BlockSpec((L, BC_WIDTH), lambda b, c: (rows(b, c), COL_BC // BC_WIDTH)),
      pl.BlockSpec((L, lanes), lambda b, c: (rows(b, c), 0)),
      full(CONV_TAPS, d), full(1, d), full(CONV_TAPS, BC_WIDTH), full(1, BC_WIDTH),
      full(1, lanes), full(1, lanes), full(1, d), full(1, d),
  ]
  return pl.pallas_call(
      _ssd_kernel,
      grid=(batch, nc),
      in_specs=in_specs,
      out_specs=pl.BlockSpec((L, d), lambda b, c: (rows(b, c), 0)),
      out_shape=jax.ShapeDtypeStruct((t, d), BF16),
      scratch_shapes=[
          pltpu.VMEM((HIST_ROWS, d), F32),
          pltpu.VMEM((HIST_ROWS, BC_WIDTH), F32),
          pltpu.VMEM((SSD_GROUPS, n, SSD_GROUP_WIDTH), F32),
      ],
      compiler_params=_params(("arbitrary", "arbitrary")),
      name="ssd_scan",
  )(proj, proj, proj, dt_raw, conv_w[:, :d], conv_b[:d].reshape(1, d), conv_w[:, d:],
    conv_b[d:].reshape(1, BC_WIDTH), jnp.pad(dt_bias, (0, pad)).reshape(1, lanes),
    jnp.pad(a_log, (0, pad)).reshape(1, lanes), d_cols, norm_w.reshape(1, d))


def _lru_kernel(x_ref, g_ref, cw_ref, cb_ref, wa_ref, ba_ref, wi_ref, bi_ref, lam_ref,
                o_ref, xh_ref, h_ref):
  ts, bd = x_ref.shape
  nt = ts // HIST_ROWS

  @pl.when(pl.program_id(2) == 0)
  def _init():
    xh_ref[...] = jnp.zeros(xh_ref.shape, F32)
    h_ref[...] = jnp.zeros(h_ref.shape, F32)

  xr3 = _causal_conv(xh_ref, x_ref[...], cw_ref, cb_ref)
  xr = xr3.reshape(ts, bd)
  xr16 = xr.astype(BF16)
  r = jax.nn.sigmoid(_dot(xr16, wa_ref[...]) + ba_ref[...])
  i = jax.nn.sigmoid(_dot(xr16, wi_ref[...]) + bi_ref[...])
  log_a = (-LRU_C) * r * _softplus(-lam_ref[...])
  a = jnp.exp(log_a)
  th = jnp.tanh(log_a)
  u = jnp.exp(0.5 * jnp.log(-2.0 * th / (1.0 - th))) * (i * xr)

  a = a.reshape(nt, HIST_ROWS, bd)
  u = u.reshape(nt, HIST_ROWS, bd)
  sub = lax.broadcasted_iota(jnp.int32, a.shape, 1)
  shift = 1
  while shift < HIST_ROWS:
    keep = sub >= shift
    a_prev = jnp.where(keep, pltpu.roll(a, shift, axis=1), 1.0)
    u_prev = jnp.where(keep, pltpu.roll(u, shift, axis=1), 0.0)
    u = a * u_prev + u
    a = a * a_prev
    shift *= 2
  carry = h_ref[...]
  hs = []
  for k in range(nt):
    hk = a[k] * carry + u[k]
    hs.append(hk)
    carry = jnp.broadcast_to(hk[HIST_ROWS - 1:HIST_ROWS, :], hk.shape)
  h_ref[...] = carry
  h = jnp.concatenate(hs, axis=0)
  o_ref[...] = (jax.nn.gelu(g_ref[...].astype(F32)) * h).astype(BF16)


def _lru_call(proj, conv_w, conv_b, w_a, b_a, w_i, b_i, lam, batch, seq, ts=512):
  t = proj.shape[0]
  bd = LRU_BLOCK_DIM
  nt = seq // ts

  def rows(b, k, s):
    return b * nt + s

  vec = lambda: pl.BlockSpec((1, bd), lambda b, k, s: (0, k))
  in_specs = [
      pl.BlockSpec((ts, bd), lambda b, k, s: (rows(b, k, s), COL_LRU_X // bd + k)),
      pl.BlockSpec((ts, bd), lambda b, k, s: (rows(b, k, s), COL_LRU_GATE // bd + k)),
      pl.BlockSpec((CONV_TAPS, bd), lambda b, k, s: (0, k)),
      vec(),
      pl.BlockSpec((None, bd, bd), lambda b, k, s: (k, 0, 0)),
      vec(),
      pl.BlockSpec((None, bd, bd), lambda b, k, s: (k, 0, 0)),
      vec(),
      vec(),
  ]
  return pl.pallas_call(
      _lru_kernel,
      grid=(batch, LRU_BLOCKS, nt),
      in_specs=in_specs,
      out_specs=pl.BlockSpec((ts, bd), lambda b, k, s: (rows(b, k, s), k)),
      out_shape=jax.ShapeDtypeStruct((t, D_MODEL), BF16),
      scratch_shapes=[pltpu.VMEM((HIST_ROWS, bd), F32), pltpu.VMEM((HIST_ROWS, bd), F32)],
      compiler_params=_params(("arbitrary", "arbitrary", "arbitrary")),
      name="rglru_scan",
  )(proj, proj, conv_w, conv_b.reshape(1, -1), w_a.astype(BF16), b_a.reshape(1, -1),
    w_i.astype(BF16), b_i.reshape(1, -1), lam.reshape(1, -1))


def _merge_kernel(ys_ref, yss_ref, yl_ref, yls_ref, ws_ref, wss_ref, wl_ref, wls_ref,
                  ga_ref, gb_ref, o_ref):
  bn = o_ref.shape[1]
  rep = bn // yss_ref.shape[1]
  pa = _dot(ys_ref[...], ws_ref[...]) * pltpu.repeat(yss_ref[...], rep, axis=1) * wss_ref[...]
  pb = _dot(yl_ref[...], wl_ref[...]) * pltpu.repeat(yls_ref[...], rep, axis=1) * wls_ref[...]
  ga = jax.nn.sigmoid(ga_ref[...].astype(F32))
  gb = jax.nn.sigmoid(gb_ref[...].astype(F32))
  o_ref[...] = (ga * pa + gb * pb).astype(o_ref.dtype)


def _merge_call(ys8, ys_scale, yl8, yl_scale, ws8, ws_scale, wl8, wl_scale, proj, bm=1024, bn=512):
  t, d = ys8.shape
  row = lambda w: pl.BlockSpec((bm, w), lambda i, j: (i, 0))
  col = lambda r: pl.BlockSpec((r, bn), lambda i, j: (0, j))
  return pl.pallas_call(
      _merge_kernel,
      grid=(t // bm, d // bn),
      in_specs=[
          row(d), row(SCALE_LANES), row(d), row(SCALE_LANES),
          col(d), col(1), col(d), col(1),
          pl.BlockSpec((bm, bn), lambda i, j: (i, COL_GATE_SSD // bn + j)),
          pl.BlockSpec((bm, bn), lambda i, j: (i, COL_GATE_LRU // bn + j)),
      ],
      out_specs=pl.BlockSpec((bm, bn), lambda i, j: (i, j)),
      out_shape=jax.ShapeDtypeStruct((t, d), BF16),
      compiler_params=_params(("arbitrary", "arbitrary")),
      name="gated_merge",
  )(ys8, ys_scale, yl8, yl_scale, ws8, ws_scale, wl8, wl_scale, proj, proj)


def _xattn_kernel(h0_ref, mix_ref, g1_ref, b1_ref, wq_ref, k_ref, v_ref, wo_ref, g_ref, b_ref,
                  o_ref, ob_ref, obt_ref, osc_ref):
  hd = XATTN_HEAD_DIM
  h = _layer_norm(DEEPNORM_ALPHA * h0_ref[...] + mix_ref[...], g1_ref[...], b1_ref[...])
  q = _dot(h.astype(BF16), wq_ref[...]).astype(BF16)
  k = k_ref[...]
  v = v_ref[...]
  outs = []
  for head in range(XATTN_HEADS):
    sl = slice(head * hd, (head + 1) * hd)
    s = _dot_nt(q[:, sl], k[:, sl]) * (hd ** -0.5)
    s = s - jnp.max(s, axis=-1, keepdims=True)
    p = jnp.exp(s)
    p = p / jnp.sum(p, axis=-1, keepdims=True)
    outs.append(_dot(p.astype(BF16), v[:, sl]))
  o = jnp.concatenate(outs, axis=1).astype(BF16)
  y = _dot(o, wo_ref[...])
  out = _layer_norm(DEEPNORM_ALPHA * h + y, g_ref[...], b_ref[...])
  o_ref[...] = out
  ob_ref[...] = out.astype(BF16)
  out_t = out.T
  amax = jnp.maximum(jnp.max(jnp.abs(out_t), axis=0, keepdims=True), FP8_AMAX_FLOOR)
  obt_ref[...] = (out_t * (FP8_TARGET / amax)).astype(FP8)
  osc_ref[0:1, :] = amax * (1.0 / FP8_TARGET)
  osc_ref[1:2, :] = jnp.sqrt(jnp.sum(out_t * out_t, axis=0, keepdims=True))


def _xattn_call(h0, mixed, g1, b1, w_q, k, v, w_o, g, b, seq, tm=256):
  t, d = h0.shape
  xw = w_q.shape[1]
  mem_len = k.shape[1]
  per_batch = seq // tm
  return pl.pallas_call(
      _xattn_kernel,
      grid=(t // tm,),
      in_specs=[
          _row_spec(tm, d), _row_spec(tm, d), _vec_spec(d), _vec_spec(d),
          pl.BlockSpec((d, xw), lambda i: (0, 0)),
          pl.BlockSpec((None, mem_len, xw), lambda i: (i // per_batch, 0, 0)),
          pl.BlockSpec((None, mem_len, xw), lambda i: (i // per_batch, 0, 0)),
          pl.BlockSpec((xw, d), lambda i: (0, 0)),
          _vec_spec(d), _vec_spec(d),
      ],
      out_specs=[_row_spec(tm, d), _row_spec(tm, d), pl.BlockSpec((d, tm), lambda i: (0, i)),
                 pl.BlockSpec((2, tm), lambda i: (0, i))],
      out_shape=[jax.ShapeDtypeStruct((t, d), F32), jax.ShapeDtypeStruct((t, d), BF16),
                 jax.ShapeDtypeStruct((d, t), FP8), jax.ShapeDtypeStruct((2, t), F32)],
      compiler_params=_params(("arbitrary",)),
      name="ln1_mem_xattn_ln2",
  )(h0, mixed, g1.reshape(1, d), b1.reshape(1, d), w_q, k, v, w_o, g.reshape(1, d),
    b.reshape(1, d))


_PEER_CAND_PAIRS = tuple((a, b) for a in range(PEER_TOPK) for b in range(PEER_TOPK)
                         if (a + 1) * (b + 1) <= PEER_TOPK)
_PEER_CAND_ROWS = 56


def _peer_route_kernel(q_ref, k1_ref, k2_ref, s1_ref, s2_ref, st_ref, cand_ref):
  tm = q_ref.shape[0]
  kd = PEER_N_KEYS
  neg_inf = -jnp.inf

  def top_values(s):
    vals = []
    cur = s
    for _ in range(PEER_TOPK):
      m = jnp.max(cur, axis=0, keepdims=True)
      vals.append(m)
      cur = jnp.where(cur == m, neg_inf, cur)
    return vals

  cand_ref[...] = jnp.full(cand_ref.shape, neg_inf, F32)
  for head in range(PEER_HEADS):
    q1 = q_ref[:, head * 2 * kd:head * 2 * kd + kd]
    q2 = q_ref[:, head * 2 * kd + kd:(head + 1) * 2 * kd]
    s1 = _dot_nt(k1_ref[head], q1)
    s2 = _dot_nt(k2_ref[head], q2)
    s1_ref[head] = s1
    s2_ref[head] = s2
    v1 = top_values(s1)
    v2 = top_values(s2)
    for idx, (a, b) in enumerate(_PEER_CAND_PAIRS):
      cand_ref[idx:idx + 1, :] = v1[a] + v2[b]
    cand = cand_ref[...]
    cur = cand
    count = jnp.zeros((1, tm), F32)
    tau = jnp.full((1, tm), neg_inf, F32)
    for _ in range(PEER_TOPK):
      m = jnp.max(cur, axis=0, keepdims=True)
      eq = cur == m
      new_count = count + jnp.sum(jnp.where(eq, 1.0, 0.0), axis=0, keepdims=True)
      tau = jnp.where(count < PEER_TOPK, jnp.where(new_count >= PEER_TOPK, m, tau), tau)
      count = new_count
      cur = jnp.where(eq, neg_inf, cur)
    top = v1[0] + v2[0]
    z = jnp.sum(jnp.where(cand >= tau, jnp.exp(cand - top), 0.0), axis=0, keepdims=True)
    st_ref[0, head:head + 1, :] = tau
    st_ref[1, head:head + 1, :] = v1[0]
    st_ref[2, head:head + 1, :] = v2[0]
    st_ref[3, head:head + 1, :] = 1.0 / z


def _peer_route_call(q, keys_1, keys_2, tm=256):
  t = q.shape[0]
  h, kd = PEER_HEADS, PEER_N_KEYS
  score_spec = pl.BlockSpec((h, kd, tm), lambda i: (0, 0, i))
  return pl.pallas_call(
      _peer_route_kernel,
      grid=(t // tm,),
      in_specs=[pl.BlockSpec((tm, q.shape[1]), lambda i: (i, 0)),
                pl.BlockSpec((h, kd, kd), lambda i: (0, 0, 0)),
                pl.BlockSpec((h, kd, kd), lambda i: (0, 0, 0))],
      out_specs=[score_spec, score_spec, pl.BlockSpec((4, h, tm), lambda i: (0, 0, i))],
      out_shape=[jax.ShapeDtypeStruct((h, kd, t), F32), jax.ShapeDtypeStruct((h, kd, t), F32),
                 jax.ShapeDtypeStruct((4, h, t), F32)],
      scratch_shapes=[pltpu.VMEM((_PEER_CAND_ROWS, tm), F32)],
      compiler_params=_params(("arbitrary",)),
      name="peer_route",
  )(q, keys_1, keys_2)


BF16_ROWS = 16


def _peer_dense_kernel(xt_ref, tsc_ref, u_ref, v_ref, s1_ref, s2_ref, st_ref, o_ref,
                       d2_ref, e2_ref):
  j = pl.program_id(1)
  tm = xt_ref.shape[1]
  half = u_ref.shape[0] // 2
  kd = PEER_N_KEYS
  rows_per_half = half // kd
  tiles = kd // BF16_ROWS

  def packed(x):
    return x.astype(BF16).reshape(tiles, BF16_ROWS, tm)

  def packed_row(row):
    return jnp.broadcast_to(row, (BF16_ROWS, tm)).astype(BF16)[None]

  @pl.when(j == 0)
  def _init():
    o_ref[...] = jnp.zeros(o_ref.shape, F32)
    for head in range(PEER_HEADS):
      d2_ref[head] = packed(s2_ref[head] - st_ref[0, head:head + 1, :])
      e2_ref[head] = packed(jnp.exp(s2_ref[head] - st_ref[2, head:head + 1, :]))

  def gate_term(i1, head):
    s1_row = s1_ref[head, pl.ds(i1, 1), :]
    e1_row = (jnp.exp(s1_row - st_ref[1, head:head + 1, :]) * st_ref[3, head:head + 1, :]
              * tsc_ref[1:2, :])
    sel = d2_ref[head] >= packed_row(-s1_row)
    return jnp.where(sel, e2_ref[head] * packed_row(e1_row), jnp.zeros((), BF16))

  def half_tile(half_index):
    rows = slice(half_index * half, (half_index + 1) * half)
    act = _dot(u_ref[rows, :], xt_ref[...]) * tsc_ref[0:1, :]
    gates = []
    for r in range(rows_per_half):
      i1 = (2 * j + half_index) * rows_per_half + r
      g = gate_term(i1, 0)
      for head in range(1, PEER_HEADS):
        g = g + gate_term(i1, head)
      gates.append(g.reshape(kd, tm))
    p = jnp.concatenate(gates, axis=0) * jax.nn.gelu(act.astype(BF16))
    o_ref[...] += _dot(p.astype(F32).T.astype(FP8), v_ref[rows, :])

  half_tile(0)
  half_tile(1)


def _peer_dense_call(xt, tsc, u, v, s1, s2, stats, tm=512, te=1024):
  d, t = xt.shape
  e = u.shape[0]
  h, kd = PEER_HEADS, PEER_N_KEYS
  score_spec = pl.BlockSpec((h, kd, tm), lambda i, j: (0, 0, i))
  packed = pltpu.VMEM((h, kd // BF16_ROWS, BF16_ROWS, tm), BF16)
  return pl.pallas_call(
      _peer_dense_kernel,
      grid=(t // tm, e // te),
      in_specs=[pl.BlockSpec((d, tm), lambda i, j: (0, i)),
                pl.BlockSpec((2, tm), lambda i, j: (0, i)),
                pl.BlockSpec((te, d), lambda i, j: (j, 0)),
                pl.BlockSpec((te, d), lambda i, j: (j, 0)),
                score_spec, score_spec,
                pl.BlockSpec((4, h, tm), lambda i, j: (0, 0, i))],
      out_specs=pl.BlockSpec((tm, d), lambda i, j: (i, 0)),
      out_shape=jax.ShapeDtypeStruct((t, d), F32),
      scratch_shapes=[packed, packed],
      compiler_params=_params(("arbitrary", "arbitrary")),
      name="peer_dense",
  )(xt, tsc, u, v, s1, s2, stats)


def kernel(x, mem, ln_in_g, ln_in_b, w_in, ssd_conv_w, ssd_conv_b, ssd_dt_bias, ssd_a_log, ssd_d, ssd_norm_w, lru_conv_w, lru_conv_b, lru_w_a, lru_b_a, lru_w_i, lru_b_i, lru_lambda, w_proj_ssd, w_proj_lru, w_mix_out, ln1_g, ln1_b, xa_w_q, xa_w_k, xa_w_v, xa_w_o, ln2_g, ln2_b, peer_w_q, peer_keys_1, peer_keys_2, peer_u, peer_v, ln3_g, ln3_b):
  batch, seq, d = x.shape
  t = batch * seq
  assert w_in.shape[0] == 1, "single-layer trunk"

  w = w_in[0]
  c_dt = SSD_SLAB_COLS
  c_lru = c_dt + SSD_HEADS
  w_dt = jnp.pad(w[:, c_dt:c_lru], ((0, 0), (0, 128 - SSD_HEADS))).astype(BF16)

  w_amax = _col_amax(w)
  w8, w_scale = _quantize_cols(w, w_amax)
  w_ssd8, w_ssd_scale = w8[:, :c_dt], w_scale[:, :c_dt]
  w_lru8, w_lru_scale = w8[:, c_lru:], w_scale[:, c_lru:]

  h0, h0b, h0q, h0_scale = _ln_call(x.reshape(t, d), ln_in_g, ln_in_b)
  proj_ssd = _matmul_fp8(h0q, h0_scale, w_ssd8, w_ssd_scale, BF16, "in_proj_ssd")
  proj_lru = _matmul_fp8(h0q, h0_scale, w_lru8, w_lru_scale, BF16, "in_proj_lru")
  dt_raw = _matmul(h0b, w_dt, F32, "dt_proj")

  y_ssd = _ssd_call(proj_ssd, dt_raw, ssd_conv_w[0], ssd_conv_b[0], ssd_dt_bias[0], ssd_a_log[0],
                    ssd_d[0], ssd_norm_w[0], batch, seq)
  y_lru = _lru_call(proj_lru, lru_conv_w[0], lru_conv_b[0], lru_w_a[0], lru_b_a[0], lru_w_i[0],
                    lru_b_i[0], lru_lambda[0], batch, seq)
  ys8, ys_scale = _quantize_call(y_ssd, "quant_y_ssd")
  yl8, yl_scale = _quantize_call(y_lru, "quant_y_lru")
  wps8, wps_scale = _quantize_cols(w_proj_ssd[0], _col_amax(w_proj_ssd[0]))
  wpl8, wpl_scale = _quantize_cols(w_proj_lru[0], _col_amax(w_proj_lru[0]))
  merged = _merge_call(ys8, ys_scale, yl8, yl_scale, wps8, wps_scale, wpl8, wpl_scale, proj_lru)
  mg8, mg_scale = _quantize_call(merged, "quant_merged")
  wmx8, wmx_scale = _quantize_cols(w_mix_out[0], _col_amax(w_mix_out[0]))
  mixed = _matmul_fp8(mg8, mg_scale, wmx8, wmx_scale, F32, "mix_out")

  mem_len = mem.shape[1]
  w_kv = jnp.concatenate([xa_w_k[0], xa_w_v[0]], axis=1).astype(BF16)
  kv = _matmul(mem.reshape(batch * mem_len, d).astype(BF16), w_kv, BF16, "mem_kv")
  xw = xa_w_k.shape[-1]
  k = kv[:, :xw].reshape(batch, mem_len, xw)
  v = kv[:, xw:].reshape(batch, mem_len, xw)
  h2, h2b, h2t8, h2t_stats = _xattn_call(h0, mixed, ln1_g[0], ln1_b[0], xa_w_q[0].astype(BF16),
                                         k, v, xa_w_o[0].astype(BF16), ln2_g[0], ln2_b[0], seq)

  q = _matmul(h2b, peer_w_q[0].astype(BF16), BF16, "peer_query")
  s1, s2, stats = _peer_route_call(q, peer_keys_1[0].astype(BF16), peer_keys_2[0].astype(BF16))
  u, v_tab = peer_u[0], peer_v[0]
  u_amax = jnp.maximum(jnp.max(jnp.abs(u)), FP8_AMAX_FLOOR)
  v_amax = jnp.maximum(jnp.max(jnp.abs(v_tab)), FP8_AMAX_FLOOR)
  u8 = (u * (FP8_TARGET / u_amax)).astype(FP8)
  v8 = (v_tab * (FP8_TARGET / v_amax)).astype(FP8)
  u_norm_max = jnp.sqrt(jnp.max(jnp.sum(u * u, axis=1)))
  p_bound = (FP8_ROUNDING_SLACK * u_norm_max
             * jnp.sum(stats[3], axis=0, keepdims=True) * h2t_stats[1:2, :])
  p_bound = jnp.maximum(p_bound, FP8_AMAX_FLOOR)
  tok_scales = jnp.concatenate([h2t_stats[0:1, :] * (u_amax * (1.0 / FP8_TARGET)),
                                FP8_TARGET / p_bound], axis=0)
  ffn_scale = jnp.broadcast_to(
      (p_bound * (v_amax * (1.0 / FP8_TARGET) ** 2)).reshape(t, 1), (t, SCALE_LANES))
  ffn = _peer_dense_call(h2t8, tok_scales, u8, v8, s1, s2, stats)
  h3, _ = _res_ln_call(h2, ffn, ln3_g[0], ln3_b[0], "res_ln3", y_scale=ffn_scale)
  return h3.reshape(batch, seq, d)
```
